```python
import math
import jax, jax.numpy as jnp
from jax import lax
import numpy as np

D_MODEL = 2048
BATCH = 8
SEQ = 2048
DEPTH = 2

D_MIX = D_MODEL
D_SSM = D_MIX // 2
D_ATTN = D_MIX - D_SSM
SSM_GROUP = 16
N_SSM_GROUPS = D_SSM // SSM_GROUP
SSM_STATE = 64
ATTN_HEAD_DIM = 64
ATTN_V_DIM = 2 * ATTN_HEAD_DIM
N_ATTN_HEADS = D_ATTN // ATTN_V_DIM
QK_WIDTH = N_ATTN_HEADS * 2 * ATTN_HEAD_DIM
ROT_DIM = ATTN_HEAD_DIM // 4
ROPE_THETA = 500000.0
Q_BLOCK = 128
DT_MIN = 0.001
DT_MAX = 0.1
LN_EPS = 1e-5
RMS_EPS = 1e-5
DEEPNORM_ALPHA = (2.0 * DEPTH) ** 0.25
DEEPNORM_BETA = (8.0 * DEPTH) ** -0.25
PROJ_SIZES = (D_SSM, D_SSM, QK_WIDTH, QK_WIDTH, D_ATTN, D_ATTN)
D_IN_PROJ = sum(PROJ_SIZES)
PROJ_SPLITS = tuple(int(s) for s in np.cumsum(PROJ_SIZES)[:-1])

kernel_name = "hymba_s5_diffattn_deepnorm_encoder"


def layer_norm(x, g, b):
    xf = x.astype(jnp.float32)
    mu = jnp.mean(xf, axis=-1, keepdims=True)
    var = jnp.mean(jnp.square(xf - mu), axis=-1, keepdims=True)
    y = (xf - mu) * lax.rsqrt(var + LN_EPS) * g.astype(jnp.float32) + b.astype(jnp.float32)
    return y.astype(x.dtype)


def rms_norm(x, g):
    xf = x.astype(jnp.float32)
    y = xf * lax.rsqrt(jnp.mean(jnp.square(xf), axis=-1, keepdims=True) + RMS_EPS)
    return y * g.astype(jnp.float32)


def partial_rotary(t, cos, sin):
    half = ROT_DIM // 2
    r1 = t[..., :half]
    r2 = t[..., half:ROT_DIM]
    rest = t[..., ROT_DIM:]
    c = cos[None, :, None, None, :]
    s = sin[None, :, None, None, :]
    return jnp.concatenate([r1 * c - r2 * s, r2 * c + r1 * s, rest], axis=-1)


def _ssm_combine(e1, e2):
    a1r, a1i, b1r, b1i = e1
    a2r, a2i, b2r, b2i = e2
    return (a2r * a1r - a2i * a1i,
            a2r * a1i + a2i * a1r,
            a2r * b1r - a2i * b1i + b2r,
            a2r * b1i + a2i * b1r + b2i)


def s5_scan(u, lam_re, lam_im, log_step, b_re, b_im, c_re, c_im, reverse):
    step = jnp.exp(log_step)[:, None]
    zr = lam_re * step
    zi = lam_im * step
    mag = jnp.exp(zr)
    ab_re = mag * jnp.cos(zi)
    ab_im = mag * jnp.sin(zi)
    nr = ab_re - 1.0
    ni = ab_im
    den = lam_re * lam_re + lam_im * lam_im
    coef_re = (nr * lam_re + ni * lam_im) / den
    coef_im = (ni * lam_re - nr * lam_im) / den
    bb_re = coef_re[..., None] * b_re - coef_im[..., None] * b_im
    bb_im = coef_re[..., None] * b_im + coef_im[..., None] * b_re
    bu_re = jnp.einsum('bsgc,gpc->bsgp', u, bb_re)
    bu_im = jnp.einsum('bsgc,gpc->bsgp', u, bb_im)
    a_re = jnp.broadcast_to(ab_re, bu_re.shape)
    a_im = jnp.broadcast_to(ab_im, bu_im.shape)
    _, _, x_re, x_im = lax.associative_scan(
        _ssm_combine, (a_re, a_im, bu_re, bu_im), reverse=reverse, axis=1)
    return (jnp.einsum('bsgp,gcp->bsgc', x_re, c_re)
            - jnp.einsum('bsgp,gcp->bsgc', x_im, c_im))


def diff_attention(q, k, v, lam):
    b, s, h, _, d = q.shape
    nb = s // Q_BLOCK
    scale = d ** -0.5
    qb = q.reshape(b, nb, Q_BLOCK, h, 2, d).transpose(1, 0, 3, 4, 2, 5)
    kt = k.transpose(0, 2, 3, 1, 4)
    vt = v.transpose(0, 2, 1, 3)

    def block(q_blk):
        sc = jnp.einsum('bhmqd,bhmkd->bhmqk', q_blk, kt).astype(jnp.float32) * scale
        p = jax.nn.softmax(sc, axis=-1)
        w = p[:, :, 0] - lam * p[:, :, 1]
        return jnp.einsum('bhqk,bhkv->bhqv', w.astype(vt.dtype), vt)

    o = lax.map(block, qb)
    return o.transpose(1, 0, 3, 2, 4).reshape(b, s, h, 2 * d)


def setup_inputs(seed: int = 0) -> dict:
    key = jax.random.key(seed)
    ks = jax.random.split(key, 24)
    f32 = jnp.float32
    G, P, Cg = N_SSM_GROUPS, SSM_STATE, SSM_GROUP
    nrm = lambda k, shp: jax.random.normal(k, shp, f32)
    x = nrm(ks[0], (BATCH, SEQ, D_MODEL))
    ln_emb_g = 1.0 + 0.02 * nrm(ks[1], (D_MODEL,))
    ln_emb_b = 0.02 * nrm(ks[2], (D_MODEL,))
    w_in = nrm(ks[3], (DEPTH, D_MODEL, D_IN_PROJ)) * D_MODEL ** -0.5
    ssm_lam_re = -0.5 + 0.01 * nrm(ks[4], (DEPTH, 2, G, P))
    ssm_lam_im = (math.pi * jnp.arange(P, dtype=f32))[None, None, None, :] \
        + 0.0 * nrm(ks[5], (DEPTH, 2, G, P)) * 0.0 + jnp.zeros((DEPTH, 2, G, P), f32)
    ssm_log_step = jax.random.uniform(ks[6], (DEPTH, 2, G), f32,
                                      minval=math.log(DT_MIN), maxval=math.log(DT_MAX))
    ssm_b_re = nrm(ks[7], (DEPTH, 2, G, P, Cg)) * (2.0 * Cg) ** -0.5
    ssm_b_im = nrm(ks[8], (DEPTH, 2, G, P, Cg)) * (2.0 * Cg) ** -0.5
    ssm_c_re = nrm(ks[9], (DEPTH, 2, G, Cg, P)) * P ** -0.5
    ssm_c_im = nrm(ks[10], (DEPTH, 2, G, Cg, P)) * P ** -0.5
    ssm_d = nrm(ks[11], (DEPTH, D_SSM))
    w_glu = nrm(ks[12], (DEPTH, D_SSM, D_SSM)) * D_SSM ** -0.5
    b_glu = 0.02 * nrm(ks[13], (DEPTH, D_SSM))
    lambda_q1 = 0.1 * nrm(ks[14], (DEPTH, ATTN_HEAD_DIM))
    lambda_k1 = 0.1 * nrm(ks[15], (DEPTH, ATTN_HEAD_DIM))
    lambda_q2 = 0.1 * nrm(ks[16], (DEPTH, ATTN_HEAD_DIM))
    lambda_k2 = 0.1 * nrm(ks[17], (DEPTH, ATTN_HEAD_DIM))
    attn_norm_g = 1.0 + 0.02 * nrm(ks[18], (DEPTH, ATTN_V_DIM))
    w_out = nrm(ks[19], (DEPTH, D_MIX, D_MODEL)) * (D_MIX ** -0.5) * DEEPNORM_BETA
    ln_g = 1.0 + 0.02 * nrm(ks[20], (DEPTH, D_MODEL))
    ln_b = 0.02 * nrm(ks[21], (DEPTH, D_MODEL))
    return {"x": x, "ln_emb_g": ln_emb_g, "ln_emb_b": ln_emb_b, "w_in": w_in,
            "ssm_lam_re": ssm_lam_re, "ssm_lam_im": ssm_lam_im, "ssm_log_step": ssm_log_step,
            "ssm_b_re": ssm_b_re, "ssm_b_im": ssm_b_im, "ssm_c_re": ssm_c_re, "ssm_c_im": ssm_c_im,
            "ssm_d": ssm_d, "w_glu": w_glu, "b_glu": b_glu,
            "lambda_q1": lambda_q1, "lambda_k1": lambda_k1, "lambda_q2": lambda_q2,
            "lambda_k2": lambda_k2, "attn_norm_g": attn_norm_g, "w_out": w_out,
            "ln_g": ln_g, "ln_b": ln_b}


def reference(x, ln_emb_g, ln_emb_b, w_in, ssm_lam_re, ssm_lam_im, ssm_log_step,
              ssm_b_re, ssm_b_im, ssm_c_re, ssm_c_im, ssm_d, w_glu, b_glu,
              lambda_q1, lambda_k1, lambda_q2, lambda_k2, attn_norm_g, w_out,
              ln_g, ln_b):
    f32 = jnp.float32
    b, s, _ = x.shape
    dt = x.dtype
    pos = jnp.arange(s, dtype=f32)
    inv_freq = ROPE_THETA ** (-jnp.arange(0, ROT_DIM, 2, dtype=f32) / ROT_DIM)
    ang = pos[:, None] * inv_freq[None, :]
    cos = jnp.cos(ang).astype(dt)
    sin = jnp.sin(ang).astype(dt)

    x = layer_norm(x, ln_emb_g, ln_emb_b)
    for l in range(DEPTH):
        proj = jnp.einsum('bsd,de->bse', x, w_in[l])
        u, g_ssm, q, k, v, g_attn = jnp.split(proj, PROJ_SPLITS, axis=-1)

        uf = u.astype(f32).reshape(b, s, N_SSM_GROUPS, SSM_GROUP)
        y = uf * ssm_d[l].astype(f32).reshape(N_SSM_GROUPS, SSM_GROUP)
        for direction in range(2):
            y = y + s5_scan(uf,
                            ssm_lam_re[l, direction].astype(f32), ssm_lam_im[l, direction].astype(f32),
                            ssm_log_step[l, direction].astype(f32),
                            ssm_b_re[l, direction].astype(f32), ssm_b_im[l, direction].astype(f32),
                            ssm_c_re[l, direction].astype(f32), ssm_c_im[l, direction].astype(f32),
                            reverse=(direction == 1))
        y = jax.nn.gelu(y.reshape(b, s, D_SSM)).astype(dt)
        y = y * jax.nn.sigmoid(jnp.einsum('bsc,ce->bse', y, w_glu[l]) + b_glu[l])
        y_ssm = y * jax.nn.silu(g_ssm)

        qh = partial_rotary(q.reshape(b, s, N_ATTN_HEADS, 2, ATTN_HEAD_DIM), cos, sin)
        kh = partial_rotary(k.reshape(b, s, N_ATTN_HEADS, 2, ATTN_HEAD_DIM), cos, sin)
        vh = v.reshape(b, s, N_ATTN_HEADS, ATTN_V_DIM)
        lambda_init = 0.8 - 0.6 * math.exp(-0.3 * l)
        lam = (jnp.exp(jnp.sum(lambda_q1[l].astype(f32) * lambda_k1[l].astype(f32)))
               - jnp.exp(jnp.sum(lambda_q2[l].astype(f32) * lambda_k2[l].astype(f32)))
               + lambda_init)
        o = diff_attention(qh, kh, vh, lam)
        o = rms_norm(o, attn_norm_g[l]) * (1.0 - lambda_init)
        y_attn = o.reshape(b, s, D_ATTN).astype(dt) * jax.nn.silu(g_attn)

        mix = jnp.concatenate([y_ssm, y_attn], axis=-1)
        out = jnp.einsum('bse,ed->bsd', mix, w_out[l])
        x = layer_norm(DEEPNORM_ALPHA * x + out, ln_g[l], ln_b[l])
    return x
```

```python
import functools
import math

import jax
import jax.numpy as jnp
from jax import lax
from jax.experimental import pallas as pl
from jax.experimental.pallas import tpu as pltpu

F32 = jnp.float32
BF16 = jnp.bfloat16

D_MODEL = 2048
BATCH = 8
SEQ = 2048
DEPTH = 2
D_SSM = 1024
D_ATTN = 1024
SSM_GROUP = 16
N_SSM_GROUPS = D_SSM // SSM_GROUP
SSM_STATE = 64
HEAD_DIM = 64
V_DIM = 2 * HEAD_DIM
N_HEADS = D_ATTN // V_DIM
ROT_DIM = HEAD_DIM // 4
ROT_HALF = ROT_DIM // 2
ROPE_THETA = 500000.0
LN_EPS = 1e-5
RMS_EPS = 1e-5
DEEPNORM_ALPHA = (2.0 * DEPTH) ** 0.25
N_PROJ = 6
PROJ_W = 1024

LANES = 128
SUBLANES = 8

CHUNK = 16
N_CHUNKS = SEQ // CHUNK
GROUPS_PER_BLOCK = LANES // SSM_GROUP
N_LANE_BLOCKS = D_SSM // LANES
ROW_W = CHUNK * LANES
STATE_W = 2 * 2 * GROUPS_PER_BLOCK * SSM_STATE
STATE_TILE = 512
N_STATE_TILES = STATE_W // STATE_TILE
HALF_TILE = STATE_TILE // 2
OUT_TILE = 512
SLABS_PER_TILE = OUT_TILE // LANES
CHUNK_ROWS = N_CHUNKS * BATCH

TM_PROJ = 512
TM_GLU = 512
TM_OUT = 256
TQ = 256

VMEM_LIMIT = 56 * 1024 * 1024


def _layer_norm(x, g, b):
    mu = jnp.mean(x, axis=-1, keepdims=True)
    xc = x - mu
    var = jnp.mean(xc * xc, axis=-1, keepdims=True)
    return xc * lax.rsqrt(var + LN_EPS) * g + b


def _in_proj_kernel(x_ref, eg_ref, eb_ref, w_ref, cos_ref, sa_ref, sb_ref,
                    u_ref, gs_ref, q_ref, k_ref, v_ref, ga_ref, xs_ref, *, pre_ln):
    j = pl.program_id(1)

    @pl.when(j == 0)
    def _():
        x = x_ref[...]
        if pre_ln:
            x = _layer_norm(x, eg_ref[...], eb_ref[...])
        xs_ref[...] = x.astype(BF16)

    acc = jnp.dot(xs_ref[...], w_ref[...], preferred_element_type=F32)

    def rotary(o_ref, scale):
        c = cos_ref[...]
        sa = sa_ref[...]
        sb = sb_ref[...]
        for cb in range(PROJ_W // LANES):
            t = acc[:, cb * LANES:(cb + 1) * LANES]
            r = (t * c + pltpu.roll(t, LANES - ROT_HALF, 1) * sa
                 + pltpu.roll(t, ROT_HALF, 1) * sb)
            if scale != 1.0:
                r = r * scale
            o_ref[:, cb * LANES:(cb + 1) * LANES] = r.astype(BF16)

    @pl.when(j == 0)
    def _():
        u_ref[...] = acc

    @pl.when(j == 1)
    def _():
        gs_ref[...] = acc

    @pl.when(j == 2)
    def _():
        rotary(q_ref, HEAD_DIM ** -0.5)

    @pl.when(j == 3)
    def _():
        rotary(k_ref, 1.0)

    @pl.when(j == 4)
    def _():
        v_ref[...] = acc.astype(BF16)

    @pl.when(j == 5)
    def _():
        ga_ref[...] = acc


def _in_proj(x2d, eg, eb, w_bf16, cos_t, sin_a, sin_b, *, pre_ln):
    nt = SEQ // TM_PROJ
    n_rows = BATCH * SEQ
    tb_shape = (SEQ, BATCH * PROJ_W)
    nat_shape = (n_rows, PROJ_W)

    def tb_map(i, j):
        return (i % nt, i // nt)

    def nat_map(i, j):
        return (i, 0)

    def tab_map(i, j):
        return (i % nt, 0)

    return pl.pallas_call(
        functools.partial(_in_proj_kernel, pre_ln=pre_ln),
        grid=(n_rows // TM_PROJ, N_PROJ),
        in_specs=[
            pl.BlockSpec((TM_PROJ, D_MODEL), nat_map),
            pl.BlockSpec((1, D_MODEL), lambda i, j: (0, 0)),
            pl.BlockSpec((1, D_MODEL), lambda i, j: (0, 0)),
            pl.BlockSpec((D_MODEL, PROJ_W), lambda i, j: (0, j)),
            pl.BlockSpec((TM_PROJ, LANES), tab_map),
            pl.BlockSpec((TM_PROJ, LANES), tab_map),
            pl.BlockSpec((TM_PROJ, LANES), tab_map),
        ],
        out_specs=[
            pl.BlockSpec((TM_PROJ, PROJ_W), tb_map),
            pl.BlockSpec((TM_PROJ, PROJ_W), tb_map),
            pl.BlockSpec((TM_PROJ, PROJ_W), nat_map),
            pl.BlockSpec((TM_PROJ, PROJ_W), nat_map),
            pl.BlockSpec((TM_PROJ, PROJ_W), nat_map),
            pl.BlockSpec((TM_PROJ, PROJ_W), nat_map),
        ],
        out_shape=[
            jax.ShapeDtypeStruct(tb_shape, F32),
            jax.ShapeDtypeStruct(tb_shape, F32),
            jax.ShapeDtypeStruct(nat_shape, BF16),
            jax.ShapeDtypeStruct(nat_shape, BF16),
            jax.ShapeDtypeStruct(nat_shape, BF16),
            jax.ShapeDtypeStruct(nat_shape, F32),
        ],
        scratch_shapes=[pltpu.VMEM((TM_PROJ, D_MODEL), BF16)],
        compiler_params=pltpu.CompilerParams(
            dimension_semantics=("parallel", "arbitrary"),
            vmem_limit_bytes=VMEM_LIMIT),
        name="in_proj",
    )(x2d, eg, eb, w_bf16, cos_t, sin_a, sin_b)


def _ssm_state_kernel(u_ref, bin_ref, ar_ref, ai_ref, z_ref, h_ref, zs_ref, hs_ref):
    n = pl.program_id(1)

    @pl.when(n == 0)
    def _():
        for s in range(CHUNK):
            z_ref[:, s * LANES:(s + 1) * LANES] = (
                u_ref[:, s].reshape(CHUNK_ROWS, LANES).astype(BF16))

    zs_ref[...] = jnp.dot(z_ref[...], bin_ref[...], preferred_element_type=F32)

    ar = ar_ref[...]
    ai = ai_ref[...]
    backward = n >= N_STATE_TILES // 2

    def body(c, carry):
        hr, hi = carry
        cc = jnp.where(backward, N_CHUNKS - 1 - c, c)
        row = pl.multiple_of(cc * BATCH, SUBLANES)
        hs_ref[pl.ds(row, BATCH), 0:HALF_TILE] = hr
        hs_ref[pl.ds(row, BATCH), HALF_TILE:STATE_TILE] = hi
        zr = zs_ref[pl.ds(row, BATCH), 0:HALF_TILE]
        zi = zs_ref[pl.ds(row, BATCH), HALF_TILE:STATE_TILE]
        return (ar * hr - ai * hi + zr, ar * hi + ai * hr + zi)

    zero = jnp.zeros((BATCH, HALF_TILE), F32)
    lax.fori_loop(0, N_CHUNKS, body, (zero, zero))
    h_ref[...] = hs_ref[...].astype(BF16)


def _ssm_state(u4, w_in_state, a_re, a_im):
    return pl.pallas_call(
        _ssm_state_kernel,
        grid=(N_LANE_BLOCKS, N_STATE_TILES),
        in_specs=[
            pl.BlockSpec((N_CHUNKS, CHUNK, BATCH, LANES), lambda g, n: (0, 0, 0, g)),
            pl.BlockSpec((None, ROW_W, STATE_TILE), lambda g, n: (g, 0, n)),
            pl.BlockSpec((None, None, SUBLANES, HALF_TILE), lambda g, n: (g, n, 0, 0)),
            pl.BlockSpec((None, None, SUBLANES, HALF_TILE), lambda g, n: (g, n, 0, 0)),
        ],
        out_specs=[
            pl.BlockSpec((None, CHUNK_ROWS, ROW_W), lambda g, n: (g, 0, 0)),
            pl.BlockSpec((None, CHUNK_ROWS, STATE_TILE), lambda g, n: (g, 0, n)),
        ],
        out_shape=[
            jax.ShapeDtypeStruct((N_LANE_BLOCKS, CHUNK_ROWS, ROW_W), BF16),
            jax.ShapeDtypeStruct((N_LANE_BLOCKS, CHUNK_ROWS, STATE_W), BF16),
        ],
        scratch_shapes=[pltpu.VMEM((CHUNK_ROWS, STATE_TILE), F32),
                        pltpu.VMEM((CHUNK_ROWS, STATE_TILE), F32)],
        compiler_params=pltpu.CompilerParams(
            dimension_semantics=("parallel", "arbitrary"),
            vmem_limit_bytes=VMEM_LIMIT),
        name="ssm_state",
    )(u4, w_in_state, a_re, a_im)


def _ssm_out_kernel(z_ref, h_ref, wt_ref, wc_ref, ue_ref, d_ref, y_ref):
    y = (jnp.dot(z_ref[...], wt_ref[...], preferred_element_type=F32)
         + jnp.dot(h_ref[...], wc_ref[...], preferred_element_type=F32))
    d = d_ref[...]
    for tt in range(SLABS_PER_TILE):
        yt = y[:, tt * LANES:(tt + 1) * LANES].reshape(N_CHUNKS, BATCH, LANES)
        yt = yt + d * ue_ref[:, tt]
        y_ref[:, tt] = jax.nn.gelu(yt)


def _ssm_out(zflat, hstate, w_intra, w_out_state, u4, d_b):
    return pl.pallas_call(
        _ssm_out_kernel,
        grid=(N_LANE_BLOCKS, ROW_W // OUT_TILE),
        in_specs=[
            pl.BlockSpec((None, CHUNK_ROWS, ROW_W), lambda g, n: (g, 0, 0)),
            pl.BlockSpec((None, CHUNK_ROWS, STATE_W), lambda g, n: (g, 0, 0)),
            pl.BlockSpec((None, ROW_W, OUT_TILE), lambda g, n: (g, 0, n)),
            pl.BlockSpec((None, STATE_W, OUT_TILE), lambda g, n: (g, 0, n)),
            pl.BlockSpec((N_CHUNKS, SLABS_PER_TILE, BATCH, LANES), lambda g, n: (0, n, 0, g)),
            pl.BlockSpec((None, SUBLANES, LANES), lambda g, n: (g, 0, 0)),
        ],
        out_specs=pl.BlockSpec((N_CHUNKS, SLABS_PER_TILE, BATCH, LANES),
                               lambda g, n: (0, n, 0, g)),
        out_shape=jax.ShapeDtypeStruct((N_CHUNKS, CHUNK, BATCH, D_SSM), F32),
        compiler_params=pltpu.CompilerParams(
            dimension_semantics=("parallel", "arbitrary"),
            vmem_limit_bytes=VMEM_LIMIT),
        name="ssm_out",
    )(zflat, hstate, w_intra, w_out_state, u4, d_b)


def _glu_kernel(y_ref, g_ref, w_ref, b_ref, o_ref):
    y = y_ref[...]
    z = jnp.dot(y.astype(BF16), w_ref[...], preferred_element_type=F32) + b_ref[...]
    g = g_ref[...]
    o_ref[...] = (y * jax.nn.sigmoid(z) * (g * jax.nn.sigmoid(g))).astype(BF16)


def _glu(y2d, g2d, w_bf16, b_row):
    n_rows = BATCH * SEQ
    return pl.pallas_call(
        _glu_kernel,
        grid=(n_rows // TM_GLU,),
        in_specs=[
            pl.BlockSpec((TM_GLU, D_SSM), lambda i: (i, 0)),
            pl.BlockSpec((TM_GLU, D_SSM), lambda i: (i, 0)),
            pl.BlockSpec((D_SSM, D_SSM), lambda i: (0, 0)),
            pl.BlockSpec((1, D_SSM), lambda i: (0, 0)),
        ],
        out_specs=pl.BlockSpec((TM_GLU, D_SSM), lambda i: (i, 0)),
        out_shape=jax.ShapeDtypeStruct((n_rows, D_SSM), BF16),
        compiler_params=pltpu.CompilerParams(
            dimension_semantics=("parallel",), vmem_limit_bytes=VMEM_LIMIT),
        name="glu",
    )(y2d, g2d, w_bf16, b_row)


def _attn_kernel(lam_ref, q_ref, k_ref, v_ref, ga_ref, ng_ref, o_ref, *, post_scale):
    q = q_ref[...]
    k = k_ref[...]
    v = v_ref[...]
    lane = lax.broadcasted_iota(jnp.int32, q.shape, 1)
    zero = jnp.zeros_like(q)

    def softmax_pv(qm):
        s = lax.dot_general(qm, k, (((1,), (1,)), ((), ())), preferred_element_type=F32)
        m = jnp.max(s, axis=1, keepdims=True)
        e = jnp.exp(s - m)
        l = jnp.sum(e, axis=1, keepdims=True)
        return jnp.dot(e.astype(BF16), v, preferred_element_type=F32) / l

    o = (softmax_pv(jnp.where(lane < HEAD_DIM, q, zero))
         - lam_ref[0] * softmax_pv(jnp.where(lane >= HEAD_DIM, q, zero)))
    ms = jnp.mean(o * o, axis=1, keepdims=True)
    o = o * lax.rsqrt(ms + RMS_EPS) * ng_ref[...] * post_scale
    ga = ga_ref[...]
    o_ref[...] = (o * (ga * jax.nn.sigmoid(ga))).astype(BF16)


def _attention(lam, q, k, v, ga, ng_row, *, post_scale):
    nq = SEQ // TQ

    def q_map(b, h, i):
        return (b * nq + i, h)

    def kv_map(b, h, i):
        return (b, h)

    return pl.pallas_call(
        functools.partial(_attn_kernel, post_scale=post_scale),
        grid=(BATCH, N_HEADS, nq),
        in_specs=[
            pl.BlockSpec(memory_space=pltpu.SMEM),
            pl.BlockSpec((TQ, V_DIM), q_map),
            pl.BlockSpec((SEQ, V_DIM), kv_map),
            pl.BlockSpec((SEQ, V_DIM), kv_map),
            pl.BlockSpec((TQ, V_DIM), q_map),
            pl.BlockSpec((1, V_DIM), lambda b, h, i: (0, 0)),
        ],
        out_specs=pl.BlockSpec((TQ, V_DIM), q_map),
        out_shape=jax.ShapeDtypeStruct((BATCH * SEQ, D_ATTN), BF16),
        compiler_params=pltpu.CompilerParams(
            dimension_semantics=("parallel", "parallel", "arbitrary"),
            vmem_limit_bytes=VMEM_LIMIT),
        name="diff_attn",
    )(lam, q, k, v, ga, ng_row)


def _out_proj_kernel(ms_ref, ma_ref, w1_ref, w2_ref, x_ref, eg_ref, eb_ref, g_ref, b_ref,
                     o_ref, *, pre_ln):
    x = x_ref[...]
    if pre_ln:
        x = _layer_norm(x, eg_ref[...], eb_ref[...])
    out = (jnp.dot(ms_ref[...], w1_ref[...], preferred_element_type=F32)
           + jnp.dot(ma_ref[...], w2_ref[...], preferred_element_type=F32))
    o_ref[...] = _layer_norm(DEEPNORM_ALPHA * x + out, g_ref[...], b_ref[...])


def _out_proj(mix_ssm_tb, mix_attn, w1, w2, x2d, eg, eb, g, b, *, pre_ln):
    nt = SEQ // TM_OUT
    n_rows = BATCH * SEQ
    row = lambda i: (0, 0)
    return pl.pallas_call(
        functools.partial(_out_proj_kernel, pre_ln=pre_ln),
        grid=(n_rows // TM_OUT,),
        in_specs=[
            pl.BlockSpec((TM_OUT, D_SSM), lambda i: (i % nt, i // nt)),
            pl.BlockSpec((TM_OUT, D_ATTN), lambda i: (i, 0)),
            pl.BlockSpec((D_SSM, D_MODEL), row),
            pl.BlockSpec((D_ATTN, D_MODEL), row),
            pl.BlockSpec((TM_OUT, D_MODEL), lambda i: (i, 0)),
            pl.BlockSpec((1, D_MODEL), row),
            pl.BlockSpec((1, D_MODEL), row),
            pl.BlockSpec((1, D_MODEL), row),
            pl.BlockSpec((1, D_MODEL), row),
        ],
        out_specs=pl.BlockSpec((TM_OUT, D_MODEL), lambda i: (i, 0)),
        out_shape=jax.ShapeDtypeStruct((n_rows, D_MODEL), F32),
        compiler_params=pltpu.CompilerParams(
            dimension_semantics=("parallel",), vmem_limit_bytes=VMEM_LIMIT),
        name="out_proj",
    )(mix_ssm_tb, mix_attn, w1, w2, x2d, eg, eb, g, b)


def _rotary_tables():
    pos = jnp.arange(SEQ, dtype=F32)
    inv_freq = ROPE_THETA ** (-jnp.arange(0, ROT_DIM, 2, dtype=F32) / ROT_DIM)
    ang = pos[:, None] * inv_freq[None, :]
    cos = jnp.cos(ang)
    sin = jnp.sin(ang)
    ones = jnp.ones((SEQ, HEAD_DIM - ROT_DIM), F32)
    zeros = jnp.zeros((SEQ, HEAD_DIM - ROT_DIM), F32)
    z8 = jnp.zeros((SEQ, ROT_HALF), F32)
    reps = LANES // HEAD_DIM
    cos_t = jnp.tile(jnp.concatenate([cos, cos, ones], axis=1), (1, reps))
    sin_a = jnp.tile(jnp.concatenate([-sin, z8, zeros], axis=1), (1, reps))
    sin_b = jnp.tile(jnp.concatenate([z8, sin, zeros], axis=1), (1, reps))
    return cos_t, sin_a, sin_b


def _ssm_operators(lam_re, lam_im, log_step, b_re, b_im, c_re, c_im):
    G, P, C, L = N_SSM_GROUPS, SSM_STATE, SSM_GROUP, CHUNK
    GB, GL = N_LANE_BLOCKS, GROUPS_PER_BLOCK
    hi = lax.Precision.HIGHEST
    step = jnp.exp(log_step)[..., None]
    zr = lam_re * step
    zi = lam_im * step
    kpow = jnp.arange(L + 1, dtype=F32)[:, None, None, None]
    mag = jnp.exp(kpow * zr[None])
    pw_re = mag * jnp.cos(kpow * zi[None])
    pw_im = mag * jnp.sin(kpow * zi[None])
    nr = pw_re[1] - 1.0
    ni = pw_im[1]
    den = lam_re * lam_re + lam_im * lam_im
    coef_re = (nr * lam_re + ni * lam_im) / den
    coef_im = (ni * lam_re - nr * lam_im) / den
    bb_re = coef_re[..., None] * b_re - coef_im[..., None] * b_im
    bb_im = coef_re[..., None] * b_im + coef_im[..., None] * b_re
    m_re = pw_re[..., None] * bb_re[None] - pw_im[..., None] * bb_im[None]
    m_im = pw_re[..., None] * bb_im[None] + pw_im[..., None] * bb_re[None]

    taps = (jnp.einsum('dgop,kdgpi->kdgio', c_re, m_re[:L], precision=hi)
            - jnp.einsum('dgop,kdgpi->kdgio', c_im, m_im[:L], precision=hi))
    s_idx = jnp.arange(L)[:, None]
    t_idx = jnp.arange(L)[None, :]
    lag_f = t_idx - s_idx
    lag_b = s_idx - t_idx
    k_f = taps[jnp.clip(lag_f, 0, L - 1), 0]
    k_b = taps[jnp.clip(lag_b, 0, L - 1), 1]
    sel = lambda m: m[:, :, None, None, None]
    k_st = jnp.where(sel(lag_f >= 0), k_f, 0.0) + jnp.where(sel(lag_b >= 0), k_b, 0.0)
    k_st = k_st.reshape(L, L, GB, GL, C, C)
    eye = jnp.eye(GL, dtype=F32)
    w_intra = jnp.einsum('stbgio,gh->bsgitho', k_st, eye).reshape(GB, ROW_W, ROW_W)

    ar_l = jnp.arange(L)
    e_re = jnp.stack([m_re[L - 1 - ar_l, 0], m_re[ar_l, 1]])
    e_im = jnp.stack([m_im[L - 1 - ar_l, 0], m_im[ar_l, 1]])
    e = jnp.stack([e_re, e_im], axis=1).reshape(2, 2, L, GB, GL, P, C)
    w_state_in = jnp.einsum('drsbgpi,gh->bsgidhrp', e, eye)
    w_state_in = w_state_in.reshape(GB, L, GL, C, 2, 2, GL // 2, 2, P)
    w_state_in = w_state_in.transpose(0, 1, 2, 3, 4, 5, 7, 6, 8).reshape(GB, ROW_W, STATE_W)

    p_re = jnp.stack([pw_re[ar_l + 1, 0], pw_re[L - ar_l, 1]])
    p_im = jnp.stack([pw_im[ar_l + 1, 0], pw_im[L - ar_l, 1]])
    cr = c_re[:, None]
    ci = c_im[:, None]
    cw_re = cr * p_re[:, :, :, None, :] - ci * p_im[:, :, :, None, :]
    cw_im = cr * p_im[:, :, :, None, :] + ci * p_re[:, :, :, None, :]
    cw = jnp.stack([cw_re, -cw_im]).reshape(2, 2, L, GB, GL, C, P)
    w_state_out = jnp.einsum('rdtbgop,gh->bdgrptho', cw, eye)
    w_state_out = w_state_out.reshape(GB, 2, 2, GL // 2, 2, P, L, GL, C)
    w_state_out = w_state_out.transpose(0, 1, 2, 4, 3, 5, 6, 7, 8).reshape(GB, STATE_W, ROW_W)

    def decay_tiles(pw):
        a = pw[L].reshape(2, GB, 2, GL // 2, P).transpose(1, 0, 2, 3, 4)
        a = a.reshape(GB, N_STATE_TILES, 1, HALF_TILE)
        return jnp.broadcast_to(a, (GB, N_STATE_TILES, SUBLANES, HALF_TILE))

    return (w_intra.astype(BF16), w_state_in.astype(BF16), w_state_out.astype(BF16),
            decay_tiles(pw_re), decay_tiles(pw_im))


def kernel(x, ln_emb_g, ln_emb_b, w_in, ssm_lam_re, ssm_lam_im, ssm_log_step, ssm_b_re,
           ssm_b_im, ssm_c_re, ssm_c_im, ssm_d, w_glu, b_glu, lambda_q1, lambda_k1,
           lambda_q2, lambda_k2, attn_norm_g, w_out, ln_g, ln_b):
    assert x.shape == (BATCH, SEQ, D_MODEL) and x.dtype == F32
    cos_t, sin_a, sin_b = _rotary_tables()
    eg = ln_emb_g.reshape(1, D_MODEL)
    eb = ln_emb_b.reshape(1, D_MODEL)
    x2d = x.reshape(BATCH * SEQ, D_MODEL)

    for l in range(DEPTH):
        pre_ln = l == 0
        u_tb, gs_tb, q, k, v, ga = _in_proj(
            x2d, eg, eb, w_in[l].astype(BF16), cos_t, sin_a, sin_b, pre_ln=pre_ln)

        w_intra, w_state_in, w_state_out, a_re, a_im = _ssm_operators(
            ssm_lam_re[l], ssm_lam_im[l], ssm_log_step[l], ssm_b_re[l], ssm_b_im[l],
            ssm_c_re[l], ssm_c_im[l])
        u4 = u_tb.reshape(N_CHUNKS, CHUNK, BATCH, D_SSM)
        zflat, hstate = _ssm_state(u4, w_state_in, a_re, a_im)
        d_b = jnp.broadcast_to(ssm_d[l].reshape(N_LANE_BLOCKS, 1, LANES),
                               (N_LANE_BLOCKS, SUBLANES, LANES))
        y4 = _ssm_out(zflat, hstate, w_intra, w_state_out, u4, d_b)
        mix_ssm = _glu(y4.reshape(SEQ * BATCH, D_SSM), gs_tb.reshape(SEQ * BATCH, D_SSM),
                       w_glu[l].astype(BF16), b_glu[l].reshape(1, D_SSM))

        lambda_init = 0.8 - 0.6 * math.exp(-0.3 * l)
        lam = (jnp.exp(jnp.sum(lambda_q1[l] * lambda_k1[l]))
               - jnp.exp(jnp.sum(lambda_q2[l] * lambda_k2[l])) + lambda_init)
        mix_attn = _attention(lam.reshape(1).astype(F32), q, k, v, ga,
                              attn_norm_g[l].reshape(1, V_DIM), post_scale=1.0 - lambda_init)

        w_o = w_out[l].astype(BF16)
        x2d = _out_proj(mix_ssm.reshape(SEQ, BATCH * D_SSM), mix_attn, w_o[:D_SSM], w_o[D_SSM:],
                        x2d, eg, eb, ln_g[l].reshape(1, D_MODEL), ln_b[l].reshape(1, D_MODEL),
                        pre_ln=pre_ln)
    return x2d.reshape(BATCH, SEQ, D_MODEL)
```

```python
import functools
import math

import jax
import jax.numpy as jnp
from jax import lax
from jax.experimental import pallas as pl
from jax.experimental.pallas import tpu as pltpu

F32 = jnp.float32
BF16 = jnp.bfloat16

D_MODEL = 2048
BATCH = 8
SEQ = 2048
DEPTH = 2
D_SSM = 1024
D_ATTN = 1024
SSM_GROUP = 16
N_SSM_GROUPS = D_SSM // SSM_GROUP
SSM_STATE = 64
HEAD_DIM = 64
V_DIM = 2 * HEAD_DIM
N_HEADS = D_ATTN // V_DIM
ROT_DIM = HEAD_DIM // 4
ROT_HALF = ROT_DIM // 2
ROPE_THETA = 500000.0
LN_EPS = 1e-5
RMS_EPS = 1e-5
DEEPNORM_ALPHA = (2.0 * DEPTH) ** 0.25
N_PROJ = 6
PROJ_W = 1024

LANES = 128
SUBLANES = 8
BF16_ROWS = 16

CHUNK = 16
N_CHUNKS = SEQ // CHUNK
N_LAGS = 2 * CHUNK - 1
GROUPS_PER_BLOCK = LANES // SSM_GROUP
N_LANE_BLOCKS = D_SSM // LANES
ROW_W = CHUNK * LANES
GROUPS_PER_TILE = LANES // SSM_STATE
N_STATE_TILES = GROUPS_PER_BLOCK // GROUPS_PER_TILE
STATE_TILE = 2 * 2 * LANES
STATE_W = N_STATE_TILES * STATE_TILE
SLABS_PER_TILE = 4
OUT_TILE = SLABS_PER_TILE * LANES
CHUNK_ROWS = N_CHUNKS * BATCH

TM_PROJ = 512
TM_GLU = 512
TM_OUT = 256
TQ = 256

VMEM_LIMIT = 56 * 1024 * 1024

_NT = (((1,), (1,)), ((), ()))


def _layer_norm(x, g, b):
    mu = jnp.mean(x, axis=-1, keepdims=True)
    xc = x - mu
    var = jnp.mean(xc * xc, axis=-1, keepdims=True)
    return xc * lax.rsqrt(var + LN_EPS) * g + b


def _in_proj_kernel(x_ref, eg_ref, eb_ref, w_ref, cos_ref, sa_ref, sb_ref,
                    u_ref, gs_ref, q_ref, k_ref, v_ref, ga_ref, xs_ref, *, pre_ln):
    j = pl.program_id(1)

    @pl.when(j == 0)
    def _():
        x = x_ref[...]
        if pre_ln:
            x = _layer_norm(x, eg_ref[...], eb_ref[...])
        xs_ref[...] = x.astype(BF16)

    acc = jnp.dot(xs_ref[...], w_ref[...], preferred_element_type=F32)

    def rotary(o_ref, scale):
        c = cos_ref[...]
        sa = sa_ref[...]
        sb = sb_ref[...]
        for cb in range(PROJ_W // LANES):
            t = acc[:, cb * LANES:(cb + 1) * LANES]
            r = (t * c + pltpu.roll(t, LANES - ROT_HALF, 1) * sa
                 + pltpu.roll(t, ROT_HALF, 1) * sb)
            if scale != 1.0:
                r = r * scale
            o_ref[:, cb * LANES:(cb + 1) * LANES] = r.astype(BF16)

    @pl.when(j == 0)
    def _():
        u_ref[...] = acc

    @pl.when(j == 1)
    def _():
        gs_ref[...] = acc

    @pl.when(j == 2)
    def _():
        rotary(q_ref, HEAD_DIM ** -0.5)

    @pl.when(j == 3)
    def _():
        rotary(k_ref, 1.0)

    @pl.when(j == 4)
    def _():
        v_ref[...] = acc.astype(BF16)

    @pl.when(j == 5)
    def _():
        ga_ref[...] = acc


def _in_proj(x2d, eg, eb, w_bf16, cos_t, sin_a, sin_b, *, pre_ln):
    nt = SEQ // TM_PROJ
    n_rows = BATCH * SEQ
    tb_shape = (SEQ, BATCH * PROJ_W)
    nat_shape = (n_rows, PROJ_W)

    def tb_map(i, j):
        return (i % nt, i // nt)

    def nat_map(i, j):
        return (i, 0)

    def tab_map(i, j):
        return (i % nt, 0)

    return pl.pallas_call(
        functools.partial(_in_proj_kernel, pre_ln=pre_ln),
        grid=(n_rows // TM_PROJ, N_PROJ),
        in_specs=[
            pl.BlockSpec((TM_PROJ, D_MODEL), nat_map),
            pl.BlockSpec((1, D_MODEL), lambda i, j: (0, 0)),
            pl.BlockSpec((1, D_MODEL), lambda i, j: (0, 0)),
            pl.BlockSpec((D_MODEL, PROJ_W), lambda i, j: (0, j)),
            pl.BlockSpec((TM_PROJ, LANES), tab_map),
            pl.BlockSpec((TM_PROJ, LANES), tab_map),
            pl.BlockSpec((TM_PROJ, LANES), tab_map),
        ],
        out_specs=[
            pl.BlockSpec((TM_PROJ, PROJ_W), tb_map),
            pl.BlockSpec((TM_PROJ, PROJ_W), tb_map),
            pl.BlockSpec((TM_PROJ, PROJ_W), nat_map),
            pl.BlockSpec((TM_PROJ, PROJ_W), nat_map),
            pl.BlockSpec((TM_PROJ, PROJ_W), nat_map),
            pl.BlockSpec((TM_PROJ, PROJ_W), nat_map),
        ],
        out_shape=[
            jax.ShapeDtypeStruct(tb_shape, F32),
            jax.ShapeDtypeStruct(tb_shape, F32),
            jax.ShapeDtypeStruct(nat_shape, BF16),
            jax.ShapeDtypeStruct(nat_shape, BF16),
            jax.ShapeDtypeStruct(nat_shape, BF16),
            jax.ShapeDtypeStruct(nat_shape, F32),
        ],
        scratch_shapes=[pltpu.VMEM((TM_PROJ, D_MODEL), BF16)],
        compiler_params=pltpu.CompilerParams(
            dimension_semantics=("parallel", "arbitrary"),
            vmem_limit_bytes=VMEM_LIMIT),
        name="in_proj",
    )(x2d, eg, eb, w_bf16, cos_t, sin_a, sin_b)


def _ssm_state_kernel(u_ref, pwc_ref, bbx_ref, a_ref, z_ref, h_ref, bt_ref, zs_ref, hs_ref):
    n = pl.program_id(1)

    @pl.when(n == 0)
    def _():
        for s in range(CHUNK):
            z_ref[:, s * LANES:(s + 1) * LANES] = (
                u_ref[:, s].reshape(CHUNK_ROWS, LANES).astype(BF16))

    row = lax.broadcasted_iota(jnp.int32, (LANES, LANES), 0)
    lane = lax.broadcasted_iota(jnp.int32, (LANES, LANES), 1)
    own = (lane // SSM_GROUP) == (n * GROUPS_PER_TILE + row // SSM_STATE)
    for d in range(2):
        b_re = jnp.where(own, bbx_ref[0, d], 0.0)
        b_im = jnp.where(own, bbx_ref[1, d], 0.0)
        for s in range(CHUNK):
            e = CHUNK - 1 - s if d == 0 else s
            p_re = pwc_ref[0, d, :, e:e + 1]
            p_im = pwc_ref[1, d, :, e:e + 1]
            r0 = d * 2 * LANES
            bt_ref[r0:r0 + LANES, s * LANES:(s + 1) * LANES] = (
                p_re * b_re - p_im * b_im).astype(BF16)
            bt_ref[r0 + LANES:r0 + 2 * LANES, s * LANES:(s + 1) * LANES] = (
                p_re * b_im + p_im * b_re).astype(BF16)

    zs_ref[...] = lax.dot_general(z_ref[...], bt_ref[...], _NT, preferred_element_type=F32)

    fa_re, fa_im = a_ref[0, 0], a_ref[1, 0]
    ba_re, ba_im = a_ref[0, 1], a_ref[1, 1]

    def body(c, carry):
        f_re, f_im, b_re, b_im = carry
        rf = pl.multiple_of(c * BATCH, SUBLANES)
        rb = pl.multiple_of((N_CHUNKS - 1 - c) * BATCH, SUBLANES)
        hs_ref[pl.ds(rf, BATCH), 0:LANES] = f_re
        hs_ref[pl.ds(rf, BATCH), LANES:2 * LANES] = f_im
        hs_ref[pl.ds(rb, BATCH), 2 * LANES:3 * LANES] = b_re
        hs_ref[pl.ds(rb, BATCH), 3 * LANES:4 * LANES] = b_im
        zf_re = zs_ref[pl.ds(rf, BATCH), 0:LANES]
        zf_im = zs_ref[pl.ds(rf, BATCH), LANES:2 * LANES]
        zb_re = zs_ref[pl.ds(rb, BATCH), 2 * LANES:3 * LANES]
        zb_im = zs_ref[pl.ds(rb, BATCH), 3 * LANES:4 * LANES]
        return (fa_re * f_re - fa_im * f_im + zf_re, fa_re * f_im + fa_im * f_re + zf_im,
                ba_re * b_re - ba_im * b_im + zb_re, ba_re * b_im + ba_im * b_re + zb_im)

    zero = jnp.zeros((BATCH, LANES), F32)
    lax.fori_loop(0, N_CHUNKS, body, (zero, zero, zero, zero))
    h_ref[...] = hs_ref[...].astype(BF16)


def _ssm_state(u4, pwc, bbx, a16):
    tile_map = lambda g, n: (g, n, 0, 0, 0, 0)
    return pl.pallas_call(
        _ssm_state_kernel,
        grid=(N_LANE_BLOCKS, N_STATE_TILES),
        in_specs=[
            pl.BlockSpec((N_CHUNKS, CHUNK, BATCH, LANES), lambda g, n: (0, 0, 0, g)),
            pl.BlockSpec((None, None, 2, 2, LANES, CHUNK + 1), tile_map),
            pl.BlockSpec((None, None, 2, 2, LANES, LANES), tile_map),
            pl.BlockSpec((None, None, 2, 2, SUBLANES, LANES), tile_map),
        ],
        out_specs=[
            pl.BlockSpec((None, CHUNK_ROWS, ROW_W), lambda g, n: (g, 0, 0)),
            pl.BlockSpec((None, CHUNK_ROWS, STATE_TILE), lambda g, n: (g, 0, n)),
        ],
        out_shape=[
            jax.ShapeDtypeStruct((N_LANE_BLOCKS, CHUNK_ROWS, ROW_W), BF16),
            jax.ShapeDtypeStruct((N_LANE_BLOCKS, CHUNK_ROWS, STATE_W), BF16),
        ],
        scratch_shapes=[pltpu.VMEM((STATE_TILE, ROW_W), BF16),
                        pltpu.VMEM((CHUNK_ROWS, STATE_TILE), F32),
                        pltpu.VMEM((CHUNK_ROWS, STATE_TILE), F32)],
        compiler_params=pltpu.CompilerParams(
            dimension_semantics=("arbitrary", "arbitrary"),
            vmem_limit_bytes=VMEM_LIMIT),
        name="ssm_state",
    )(u4, pwc, bbx, a16)


def _ssm_out_kernel(z_ref, h_ref, tap_ref, rep_ref, cdup_ref, pwt_ref, ue_ref, d_ref, y_ref,
                    blk_ref, wt_ref, wc_ref):
    g = pl.program_id(0)
    n = pl.program_id(1)

    @pl.when(n == 0)
    def _():
        x = jnp.dot(tap_ref[...], rep_ref[...], preferred_element_type=F32)
        row = lax.broadcasted_iota(jnp.int32, x.shape, 0)
        lane = lax.broadcasted_iota(jnp.int32, x.shape, 1)
        own = ((row % LANES) // SSM_GROUP) == (lane // SSM_GROUP)
        blk_ref[...] = jnp.where(own, x, 0.0).astype(BF16)

    @pl.when((g == 0) & (n == 0))
    def _():
        wc_ref[...] = jnp.zeros_like(wc_ref)

    for tt in range(SLABS_PER_TILE):
        t = n * SLABS_PER_TILE + tt
        for s in range(CHUNK):
            off = pl.multiple_of((t - s + CHUNK - 1) * LANES, LANES)
            wt_ref[tt * LANES:(tt + 1) * LANES, s * LANES:(s + 1) * LANES] = (
                blk_ref[pl.ds(off, LANES), :])

    half = lax.broadcasted_iota(jnp.int32, (SSM_GROUP, LANES), 1) // SSM_STATE
    for tt in range(SLABS_PER_TILE):
        for h in range(GROUPS_PER_BLOCK):
            own = half == (h % GROUPS_PER_TILE)
            r0 = tt * LANES + h * SSM_GROUP
            for d in range(2):
                c_re = cdup_ref[0, d, h * SSM_GROUP:(h + 1) * SSM_GROUP, :]
                c_im = cdup_ref[1, d, h * SSM_GROUP:(h + 1) * SSM_GROUP, :]
                p_re = pwt_ref[tt, 0, d, h:h + 1, :]
                p_im = pwt_ref[tt, 1, d, h:h + 1, :]
                col = (h // GROUPS_PER_TILE) * STATE_TILE + d * 2 * LANES
                wc_ref[r0:r0 + SSM_GROUP, col:col + LANES] = jnp.where(
                    own, c_re * p_re - c_im * p_im, 0.0).astype(BF16)
                wc_ref[r0:r0 + SSM_GROUP, col + LANES:col + 2 * LANES] = jnp.where(
                    own, -(c_re * p_im + c_im * p_re), 0.0).astype(BF16)

    y = (lax.dot_general(z_ref[...], wt_ref[...], _NT, preferred_element_type=F32)
         + lax.dot_general(h_ref[...], wc_ref[...], _NT, preferred_element_type=F32))
    d_skip = d_ref[...]
    for tt in range(SLABS_PER_TILE):
        yt = y[:, tt * LANES:(tt + 1) * LANES].reshape(N_CHUNKS, BATCH, LANES)
        y_ref[:, tt] = jax.nn.gelu(yt + d_skip * ue_ref[:, tt])


def _ssm_out(zflat, hstate, tapr, rep, cdup, pwt, u4, d_b):
    assert SSM_GROUP == BF16_ROWS
    return pl.pallas_call(
        _ssm_out_kernel,
        grid=(N_LANE_BLOCKS, CHUNK // SLABS_PER_TILE),
        in_specs=[
            pl.BlockSpec((None, CHUNK_ROWS, ROW_W), lambda g, n: (g, 0, 0)),
            pl.BlockSpec((None, CHUNK_ROWS, STATE_W), lambda g, n: (g, 0, 0)),
            pl.BlockSpec((None, N_LAGS * LANES, SSM_GROUP), lambda g, n: (g, 0, 0)),
            pl.BlockSpec((SSM_GROUP, LANES), lambda g, n: (0, 0)),
            pl.BlockSpec((None, 2, 2, LANES, LANES), lambda g, n: (g, 0, 0, 0, 0)),
            pl.BlockSpec((None, SLABS_PER_TILE, 2, 2, GROUPS_PER_BLOCK, LANES),
                         lambda g, n: (g, n, 0, 0, 0, 0)),
            pl.BlockSpec((N_CHUNKS, SLABS_PER_TILE, BATCH, LANES), lambda g, n: (0, n, 0, g)),
            pl.BlockSpec((None, SUBLANES, LANES), lambda g, n: (g, 0, 0)),
        ],
        out_specs=pl.BlockSpec((N_CHUNKS, SLABS_PER_TILE, BATCH, LANES),
                               lambda g, n: (0, n, 0, g)),
        out_shape=jax.ShapeDtypeStruct((N_CHUNKS, CHUNK, BATCH, D_SSM), F32),
        scratch_shapes=[pltpu.VMEM((N_LAGS * LANES, LANES), BF16),
                        pltpu.VMEM((OUT_TILE, ROW_W), BF16),
                        pltpu.VMEM((OUT_TILE, STATE_W), BF16)],
        compiler_params=pltpu.CompilerParams(
            dimension_semantics=("arbitrary", "arbitrary"),
            vmem_limit_bytes=VMEM_LIMIT),
        name="ssm_out",
    )(zflat, hstate, tapr, rep, cdup, pwt, u4, d_b)


def _glu_kernel(y_ref, g_ref, w_ref, b_ref, o_ref):
    y = y_ref[...]
    z = jnp.dot(y.astype(BF16), w_ref[...], preferred_element_type=F32) + b_ref[...]
    g = g_ref[...]
    o_ref[...] = (y * jax.nn.sigmoid(z) * (g * jax.nn.sigmoid(g))).astype(BF16)


def _glu(y2d, g2d, w_bf16, b_row):
    n_rows = BATCH * SEQ
    return pl.pallas_call(
        _glu_kernel,
        grid=(n_rows // TM_GLU,),
        in_specs=[
            pl.BlockSpec((TM_GLU, D_SSM), lambda i: (i, 0)),
            pl.BlockSpec((TM_GLU, D_SSM), lambda i: (i, 0)),
            pl.BlockSpec((D_SSM, D_SSM), lambda i: (0, 0)),
            pl.BlockSpec((1, D_SSM), lambda i: (0, 0)),
        ],
        out_specs=pl.BlockSpec((TM_GLU, D_SSM), lambda i: (i, 0)),
        out_shape=jax.ShapeDtypeStruct((n_rows, D_SSM), BF16),
        compiler_params=pltpu.CompilerParams(
            dimension_semantics=("parallel",), vmem_limit_bytes=VMEM_LIMIT),
        name="glu",
    )(y2d, g2d, w_bf16, b_row)


def _attn_kernel(lam_ref, q_ref, k_ref, v_ref, ga_ref, ng_ref, o_ref, *, post_scale):
    q = q_ref[...]
    k = k_ref[...]
    v = v_ref[...]
    lane = lax.broadcasted_iota(jnp.int32, q.shape, 1)
    zero = jnp.zeros_like(q)

    def softmax_pv(qm):
        s = lax.dot_general(qm, k, _NT, preferred_element_type=F32)
        m = jnp.max(s, axis=1, keepdims=True)
        e = jnp.exp(s - m)
        l = jnp.sum(e, axis=1, keepdims=True)
        return jnp.dot(e.astype(BF16), v, preferred_element_type=F32) / l

    o = (softmax_pv(jnp.where(lane < HEAD_DIM, q, zero))
         - lam_ref[0] * softmax_pv(jnp.where(lane >= HEAD_DIM, q, zero)))
    ms = jnp.mean(o * o, axis=1, keepdims=True)
    o = o * lax.rsqrt(ms + RMS_EPS) * ng_ref[...] * post_scale
    ga = ga_ref[...]
    o_ref[...] = (o * (ga * jax.nn.sigmoid(ga))).astype(BF16)


def _attention(lam, q, k, v, ga, ng_row, *, post_scale):
    nq = SEQ // TQ

    def q_map(b, h, i):
        return (b * nq + i, h)

    def kv_map(b, h, i):
        return (b, h)

    return pl.pallas_call(
        functools.partial(_attn_kernel, post_scale=post_scale),
        grid=(BATCH, N_HEADS, nq),
        in_specs=[
            pl.BlockSpec(memory_space=pltpu.SMEM),
            pl.BlockSpec((TQ, V_DIM), q_map),
            pl.BlockSpec((SEQ, V_DIM), kv_map),
            pl.BlockSpec((SEQ, V_DIM), kv_map),
            pl.BlockSpec((TQ, V_DIM), q_map),
            pl.BlockSpec((1, V_DIM), lambda b, h, i: (0, 0)),
        ],
        out_specs=pl.BlockSpec((TQ, V_DIM), q_map),
        out_shape=jax.ShapeDtypeStruct((BATCH * SEQ, D_ATTN), BF16),
        compiler_params=pltpu.CompilerParams(
            dimension_semantics=("parallel", "parallel", "arbitrary"),
            vmem_limit_bytes=VMEM_LIMIT),
        name="diff_attn",
    )(lam, q, k, v, ga, ng_row)


def _out_proj_kernel(ms_ref, ma_ref, w1_ref, w2_ref, x_ref, eg_ref, eb_ref, g_ref, b_ref,
                     o_ref, *, pre_ln):
    x = x_ref[...]
    if pre_ln:
        x = _layer_norm(x, eg_ref[...], eb_ref[...])
    out = (jnp.dot(ms_ref[...], w1_ref[...], preferred_element_type=F32)
           + jnp.dot(ma_ref[...], w2_ref[...], preferred_element_type=F32))
    o_ref[...] = _layer_norm(DEEPNORM_ALPHA * x + out, g_ref[...], b_ref[...])


def _out_proj(mix_ssm_tb, mix_attn, w1, w2, x2d, eg, eb, g, b, *, pre_ln):
    nt = SEQ // TM_OUT
    n_rows = BATCH * SEQ
    row = lambda i: (0, 0)
    return pl.pallas_call(
        functools.partial(_out_proj_kernel, pre_ln=pre_ln),
        grid=(n_rows // TM_OUT,),
        in_specs=[
            pl.BlockSpec((TM_OUT, D_SSM), lambda i: (i % nt, i // nt)),
            pl.BlockSpec((TM_OUT, D_ATTN), lambda i: (i, 0)),
            pl.BlockSpec((D_SSM, D_MODEL), row),
            pl.BlockSpec((D_ATTN, D_MODEL), row),
            pl.BlockSpec((TM_OUT, D_MODEL), lambda i: (i, 0)),
            pl.BlockSpec((1, D_MODEL), row),
            pl.BlockSpec((1, D_MODEL), row),
            pl.BlockSpec((1, D_MODEL), row),
            pl.BlockSpec((1, D_MODEL), row),
        ],
        out_specs=pl.BlockSpec((TM_OUT, D_MODEL), lambda i: (i, 0)),
        out_shape=jax.ShapeDtypeStruct((n_rows, D_MODEL), F32),
        compiler_params=pltpu.CompilerParams(
            dimension_semantics=("parallel",), vmem_limit_bytes=VMEM_LIMIT),
        name="out_proj",
    )(mix_ssm_tb, mix_attn, w1, w2, x2d, eg, eb, g, b)


def _rotary_tables():
    pos = jnp.arange(SEQ, dtype=F32)
    inv_freq = ROPE_THETA ** (-jnp.arange(0, ROT_DIM, 2, dtype=F32) / ROT_DIM)
    ang = pos[:, None] * inv_freq[None, :]
    cos = jnp.cos(ang)
    sin = jnp.sin(ang)
    ones = jnp.ones((SEQ, HEAD_DIM - ROT_DIM), F32)
    zeros = jnp.zeros((SEQ, HEAD_DIM - ROT_DIM), F32)
    z8 = jnp.zeros((SEQ, ROT_HALF), F32)
    reps = LANES // HEAD_DIM
    cos_t = jnp.tile(jnp.concatenate([cos, cos, ones], axis=1), (1, reps))
    sin_a = jnp.tile(jnp.concatenate([-sin, z8, zeros], axis=1), (1, reps))
    sin_b = jnp.tile(jnp.concatenate([z8, sin, zeros], axis=1), (1, reps))
    return cos_t, sin_a, sin_b


def _ssm_tables(lam_re, lam_im, log_step, b_re, b_im, c_re, c_im):
    G, P, C, L = N_SSM_GROUPS, SSM_STATE, SSM_GROUP, CHUNK
    GB, NT, GT = N_LANE_BLOCKS, N_STATE_TILES, GROUPS_PER_TILE
    hi = lax.Precision.HIGHEST
    step = jnp.exp(log_step)[..., None]
    zr = lam_re * step
    zi = lam_im * step
    kpow = jnp.arange(L + 1, dtype=F32)[:, None, None, None]
    mag = jnp.exp(kpow * zr[None])
    pw = jnp.stack([mag * jnp.cos(kpow * zi[None]),
                    mag * jnp.sin(kpow * zi[None])])
    nr = pw[0, 1] - 1.0
    ni = pw[1, 1]
    den = lam_re * lam_re + lam_im * lam_im
    coef_re = (nr * lam_re + ni * lam_im) / den
    coef_im = (ni * lam_re - nr * lam_im) / den
    bb = jnp.stack([coef_re[..., None] * b_re - coef_im[..., None] * b_im,
                    coef_re[..., None] * b_im + coef_im[..., None] * b_re])

    m_re = pw[0, :L, ..., None] * bb[0][None] - pw[1, :L, ..., None] * bb[1][None]
    m_im = pw[0, :L, ..., None] * bb[1][None] + pw[1, :L, ..., None] * bb[0][None]
    taps = (jnp.einsum('dgop,kdgpi->kdgio', c_re, m_re, precision=hi)
            - jnp.einsum('dgop,kdgpi->kdgio', c_im, m_im, precision=hi))
    tap_lag = jnp.concatenate([taps[:0:-1, 1], (taps[0, 0] + taps[0, 1])[None], taps[1:, 0]])
    tapr = tap_lag.transpose(0, 1, 3, 2).reshape(N_LAGS, GB, LANES, C)
    tapr = tapr.transpose(1, 0, 2, 3).reshape(GB, N_LAGS * LANES, C).astype(BF16)

    pwc = pw.reshape(2, L + 1, 2, GB, NT, GT, P).transpose(3, 4, 0, 2, 5, 6, 1)
    pwc = pwc.reshape(GB, NT, 2, 2, LANES, L + 1)
    bbx = bb.reshape(2, 2, GB, NT, GT * P, C).transpose(2, 3, 0, 1, 4, 5)
    bbx = jnp.tile(bbx, (1, 1, 1, 1, 1, GROUPS_PER_BLOCK))
    a16 = pw[:, L].reshape(2, 2, GB, NT, 1, LANES).transpose(2, 3, 0, 1, 4, 5)
    a16 = jnp.broadcast_to(a16, (GB, NT, 2, 2, SUBLANES, LANES))

    cc = jnp.stack([c_re, c_im]).reshape(2, 2, GB, LANES, P).transpose(2, 0, 1, 3, 4)
    cdup = jnp.concatenate([cc, cc], axis=-1)
    ar_l = jnp.arange(L)
    pt = jnp.stack([pw[:, ar_l + 1, 0], pw[:, L - ar_l, 1]], axis=2)
    pt = pt.reshape(2, L, 2, GB, GROUPS_PER_BLOCK, P).transpose(3, 1, 0, 2, 4, 5)
    pwt = jnp.concatenate([pt, pt], axis=-1)
    return tapr, pwc, bbx, a16, cdup, pwt


def kernel(x, ln_emb_g, ln_emb_b, w_in, ssm_lam_re, ssm_lam_im, ssm_log_step, ssm_b_re,
           ssm_b_im, ssm_c_re, ssm_c_im, ssm_d, w_glu, b_glu, lambda_q1, lambda_k1,
           lambda_q2, lambda_k2, attn_norm_g, w_out, ln_g, ln_b):
    assert x.shape == (BATCH, SEQ, D_MODEL) and x.dtype == F32
    cos_t, sin_a, sin_b = _rotary_tables()
    eg = ln_emb_g.reshape(1, D_MODEL)
    eb = ln_emb_b.reshape(1, D_MODEL)
    x2d = x.reshape(BATCH * SEQ, D_MODEL)
    rep = (jnp.arange(LANES)[None, :] % SSM_GROUP == jnp.arange(SSM_GROUP)[:, None]).astype(BF16)

    for l in range(DEPTH):
        pre_ln = l == 0
        u_tb, gs_tb, q, k, v, ga = _in_proj(
            x2d, eg, eb, w_in[l].astype(BF16), cos_t, sin_a, sin_b, pre_ln=pre_ln)

        tapr, pwc, bbx, a16, cdup, pwt = _ssm_tables(
            ssm_lam_re[l], ssm_lam_im[l], ssm_log_step[l], ssm_b_re[l], ssm_b_im[l],
            ssm_c_re[l], ssm_c_im[l])
        u4 = u_tb.reshape(N_CHUNKS, CHUNK, BATCH, D_SSM)
        zflat, hstate = _ssm_state(u4, pwc, bbx, a16)
        d_b = jnp.broadcast_to(ssm_d[l].reshape(N_LANE_BLOCKS, 1, LANES),
                               (N_LANE_BLOCKS, SUBLANES, LANES))
        y4 = _ssm_out(zflat, hstate, tapr, rep, cdup, pwt, u4, d_b)
        mix_ssm = _glu(y4.reshape(SEQ * BATCH, D_SSM), gs_tb.reshape(SEQ * BATCH, D_SSM),
                       w_glu[l].astype(BF16), b_glu[l].reshape(1, D_SSM))

        lambda_init = 0.8 - 0.6 * math.exp(-0.3 * l)
        lam = (jnp.exp(jnp.sum(lambda_q1[l] * lambda_k1[l]))
               - jnp.exp(jnp.sum(lambda_q2[l] * lambda_k2[l])) + lambda_init)
        mix_attn = _attention(lam.reshape(1).astype(F32), q, k, v, ga,
                              attn_norm_g[l].reshape(1, V_DIM), post_scale=1.0 - lambda_init)

        w_o = w_out[l].astype(BF16)
        x2d = _out_proj(mix_ssm.reshape(SEQ, BATCH * D_SSM), mix_attn, w_o[:D_SSM], w_o[D_SSM:],
                        x2d, eg, eb, ln_g[l].reshape(1, D_MODEL), ln_b[l].reshape(1, D_MODEL),
                        pre_ln=pre_ln)
    return x2d.reshape(BATCH, SEQ, D_MODEL)
```

```python
import functools
import math

import jax
import jax.numpy as jnp
from jax import lax
from jax.experimental import pallas as pl
from jax.experimental.pallas import tpu as pltpu

F32 = jnp.float32
BF16 = jnp.bfloat16

D_MODEL = 2048
BATCH = 8
SEQ = 2048
DEPTH = 2
D_SSM = 1024
D_ATTN = 1024
SSM_GROUP = 16
N_SSM_GROUPS = D_SSM // SSM_GROUP
SSM_STATE = 64
HEAD_DIM = 64
V_DIM = 2 * HEAD_DIM
N_HEADS = D_ATTN // V_DIM
ROT_DIM = HEAD_DIM // 4
ROT_HALF = ROT_DIM // 2
ROPE_THETA = 500000.0
LN_EPS = 1e-5
RMS_EPS = 1e-5
DEEPNORM_ALPHA = (2.0 * DEPTH) ** 0.25
N_PROJ = 6
PROJ_W = 1024

LANES = 128
SUBLANES = 8
BF16_ROWS = 16

CHUNK = 16
N_CHUNKS = SEQ // CHUNK
N_LAGS = 2 * CHUNK - 1
GROUPS_PER_BLOCK = LANES // SSM_GROUP
N_LANE_BLOCKS = D_SSM // LANES
ROW_W = CHUNK * LANES
GROUPS_PER_TILE = LANES // SSM_STATE
N_STATE_TILES = GROUPS_PER_BLOCK // GROUPS_PER_TILE
STATE_TILE = 2 * 2 * LANES
STATE_W = N_STATE_TILES * STATE_TILE
SLABS_PER_TILE = 4
OUT_TILE = SLABS_PER_TILE * LANES
CHUNK_ROWS = N_CHUNKS * BATCH

TM_PROJ = 512
TM_GLU = 512
TM_OUT = 256
TQ = 256
TQ_STEP = 1024
Q_SCALE = HEAD_DIM ** -0.5 * math.log2(math.e)

VMEM_LIMIT = 56 * 1024 * 1024

_NT = (((1,), (1,)), ((), ()))


def _layer_norm(x, g, b):
    mu = jnp.mean(x, axis=-1, keepdims=True)
    xc = x - mu
    var = jnp.mean(xc * xc, axis=-1, keepdims=True)
    return xc * lax.rsqrt(var + LN_EPS) * g + b


def _in_proj_kernel(x_ref, eg_ref, eb_ref, w_ref, cos_ref, sa_ref, sb_ref,
                    u_ref, gs_ref, q_ref, k_ref, v_ref, ga_ref, xs_ref, *, pre_ln):
    j = pl.program_id(1)

    @pl.when(j == 0)
    def _():
        x = x_ref[...]
        if pre_ln:
            x = _layer_norm(x, eg_ref[...], eb_ref[...])
        xs_ref[...] = x.astype(BF16)

    acc = jnp.dot(xs_ref[...], w_ref[...], preferred_element_type=F32)

    def rotary(o_ref, scale):
        c = cos_ref[...]
        sa = sa_ref[...]
        sb = sb_ref[...]
        for cb in range(PROJ_W // LANES):
            t = acc[:, cb * LANES:(cb + 1) * LANES]
            r = (t * c + pltpu.roll(t, LANES - ROT_HALF, 1) * sa
                 + pltpu.roll(t, ROT_HALF, 1) * sb)
            if scale != 1.0:
                r = r * scale
            o_ref[:, cb * LANES:(cb + 1) * LANES] = r.astype(BF16)

    @pl.when(j == 0)
    def _():
        u_ref[...] = acc

    @pl.when(j == 1)
    def _():
        gs_ref[...] = acc

    @pl.when(j == 2)
    def _():
        rotary(q_ref, Q_SCALE)

    @pl.when(j == 3)
    def _():
        rotary(k_ref, 1.0)

    @pl.when(j == 4)
    def _():
        v_ref[...] = acc.astype(BF16)

    @pl.when(j == 5)
    def _():
        ga_ref[...] = acc


def _in_proj(x2d, eg, eb, w_bf16, cos_t, sin_a, sin_b, *, pre_ln):
    nt = SEQ // TM_PROJ
    n_rows = BATCH * SEQ
    tb_shape = (SEQ, BATCH * PROJ_W)
    nat_shape = (n_rows, PROJ_W)

    def tb_map(i, j):
        return (i % nt, i // nt)

    def nat_map(i, j):
        return (i, 0)

    def tab_map(i, j):
        return (i % nt, 0)

    return pl.pallas_call(
        functools.partial(_in_proj_kernel, pre_ln=pre_ln),
        grid=(n_rows // TM_PROJ, N_PROJ),
        in_specs=[
            pl.BlockSpec((TM_PROJ, D_MODEL), nat_map),
            pl.BlockSpec((1, D_MODEL), lambda i, j: (0, 0)),
            pl.BlockSpec((1, D_MODEL), lambda i, j: (0, 0)),
            pl.BlockSpec((D_MODEL, PROJ_W), lambda i, j: (0, j)),
            pl.BlockSpec((TM_PROJ, LANES), tab_map),
            pl.BlockSpec((TM_PROJ, LANES), tab_map),
            pl.BlockSpec((TM_PROJ, LANES), tab_map),
        ],
        out_specs=[
            pl.BlockSpec((TM_PROJ, PROJ_W), tb_map),
            pl.BlockSpec((TM_PROJ, PROJ_W), tb_map),
            pl.BlockSpec((TM_PROJ, PROJ_W), nat_map),
            pl.BlockSpec((TM_PROJ, PROJ_W), nat_map),
            pl.BlockSpec((TM_PROJ, PROJ_W), nat_map),
            pl.BlockSpec((TM_PROJ, PROJ_W), nat_map),
        ],
        out_shape=[
            jax.ShapeDtypeStruct(tb_shape, F32),
            jax.ShapeDtypeStruct(tb_shape, F32),
            jax.ShapeDtypeStruct(nat_shape, BF16),
            jax.ShapeDtypeStruct(nat_shape, BF16),
            jax.ShapeDtypeStruct(nat_shape, BF16),
            jax.ShapeDtypeStruct(nat_shape, F32),
        ],
        scratch_shapes=[pltpu.VMEM((TM_PROJ, D_MODEL), BF16)],
        compiler_params=pltpu.CompilerParams(
            dimension_semantics=("parallel", "arbitrary"),
            vmem_limit_bytes=VMEM_LIMIT),
        name="in_proj",
    )(x2d, eg, eb, w_bf16, cos_t, sin_a, sin_b)


def _ssm_state_kernel(u_ref, pwc_ref, bbx_ref, a_ref, z_ref, h_ref, bt_ref, zs_ref, hs_ref):
    n = pl.program_id(1)

    @pl.when(n == 0)
    def _():
        for s in range(CHUNK):
            z_ref[:, s * LANES:(s + 1) * LANES] = (
                u_ref[:, s].reshape(CHUNK_ROWS, LANES).astype(BF16))

    row = lax.broadcasted_iota(jnp.int32, (LANES, LANES), 0)
    lane = lax.broadcasted_iota(jnp.int32, (LANES, LANES), 1)
    own = (lane // SSM_GROUP) == (n * GROUPS_PER_TILE + row // SSM_STATE)
    for d in range(2):
        b_re = jnp.where(own, bbx_ref[0, d], 0.0)
        b_im = jnp.where(own, bbx_ref[1, d], 0.0)
        for s in range(CHUNK):
            e = CHUNK - 1 - s if d == 0 else s
            p_re = pwc_ref[0, d, :, e:e + 1]
            p_im = pwc_ref[1, d, :, e:e + 1]
            r0 = d * 2 * LANES
            bt_ref[r0:r0 + LANES, s * LANES:(s + 1) * LANES] = (
                p_re * b_re - p_im * b_im).astype(BF16)
            bt_ref[r0 + LANES:r0 + 2 * LANES, s * LANES:(s + 1) * LANES] = (
                p_re * b_im + p_im * b_re).astype(BF16)

    zs_ref[...] = lax.dot_general(z_ref[...], bt_ref[...], _NT, preferred_element_type=F32)

    fa_re, fa_im = a_ref[0, 0], a_ref[1, 0]
    ba_re, ba_im = a_ref[0, 1], a_ref[1, 1]

    def body(c, carry):
        f_re, f_im, b_re, b_im = carry
        rf = pl.multiple_of(c * BATCH, SUBLANES)
        rb = pl.multiple_of((N_CHUNKS - 1 - c) * BATCH, SUBLANES)
        hs_ref[pl.ds(rf, BATCH), 0:LANES] = f_re
        hs_ref[pl.ds(rf, BATCH), LANES:2 * LANES] = f_im
        hs_ref[pl.ds(rb, BATCH), 2 * LANES:3 * LANES] = b_re
        hs_ref[pl.ds(rb, BATCH), 3 * LANES:4 * LANES] = b_im
        zf_re = zs_ref[pl.ds(rf, BATCH), 0:LANES]
        zf_im = zs_ref[pl.ds(rf, BATCH), LANES:2 * LANES]
        zb_re = zs_ref[pl.ds(rb, BATCH), 2 * LANES:3 * LANES]
        zb_im = zs_ref[pl.ds(rb, BATCH), 3 * LANES:4 * LANES]
        return (fa_re * f_re - fa_im * f_im + zf_re, fa_re * f_im + fa_im * f_re + zf_im,
                ba_re * b_re - ba_im * b_im + zb_re, ba_re * b_im + ba_im * b_re + zb_im)

    zero = jnp.zeros((BATCH, LANES), F32)
    lax.fori_loop(0, N_CHUNKS, body, (zero, zero, zero, zero))
    h_ref[...] = hs_ref[...].astype(BF16)


def _ssm_state(u4, pwc, bbx, a16):
    tile_map = lambda g, n: (g, n, 0, 0, 0, 0)
    return pl.pallas_call(
        _ssm_state_kernel,
        grid=(N_LANE_BLOCKS, N_STATE_TILES),
        in_specs=[
            pl.BlockSpec((N_CHUNKS, CHUNK, BATCH, LANES), lambda g, n: (0, 0, 0, g)),
            pl.BlockSpec((None, None, 2, 2, LANES, CHUNK + 1), tile_map),
            pl.BlockSpec((None, None, 2, 2, LANES, LANES), tile_map),
            pl.BlockSpec((None, None, 2, 2, SUBLANES, LANES), tile_map),
        ],
        out_specs=[
            pl.BlockSpec((None, CHUNK_ROWS, ROW_W), lambda g, n: (g, 0, 0)),
            pl.BlockSpec((None, CHUNK_ROWS, STATE_TILE), lambda g, n: (g, 0, n)),
        ],
        out_shape=[
            jax.ShapeDtypeStruct((N_LANE_BLOCKS, CHUNK_ROWS, ROW_W), BF16),
            jax.ShapeDtypeStruct((N_LANE_BLOCKS, CHUNK_ROWS, STATE_W), BF16),
        ],
        scratch_shapes=[pltpu.VMEM((STATE_TILE, ROW_W), BF16),
                        pltpu.VMEM((CHUNK_ROWS, STATE_TILE), F32),
                        pltpu.VMEM((CHUNK_ROWS, STATE_TILE), F32)],
        compiler_params=pltpu.CompilerParams(
            dimension_semantics=("arbitrary", "arbitrary"),
            vmem_limit_bytes=VMEM_LIMIT),
        name="ssm_state",
    )(u4, pwc, bbx, a16)


def _ssm_out_kernel(z_ref, h_ref, tap_ref, rep_ref, cdup_ref, pwt_ref, ue_ref, d_ref, y_ref,
                    blk_ref, wt_ref, wc_ref):
    g = pl.program_id(0)
    n = pl.program_id(1)

    @pl.when(n == 0)
    def _():
        x = jnp.dot(tap_ref[...], rep_ref[...], preferred_element_type=F32)
        row = lax.broadcasted_iota(jnp.int32, x.shape, 0)
        lane = lax.broadcasted_iota(jnp.int32, x.shape, 1)
        own = ((row % LANES) // SSM_GROUP) == (lane // SSM_GROUP)
        blk_ref[...] = jnp.where(own, x, 0.0).astype(BF16)

    @pl.when((g == 0) & (n == 0))
    def _():
        wc_ref[...] = jnp.zeros_like(wc_ref)

    for tt in range(SLABS_PER_TILE):
        t = n * SLABS_PER_TILE + tt
        for s in range(CHUNK):
            off = pl.multiple_of((t - s + CHUNK - 1) * LANES, LANES)
            wt_ref[tt * LANES:(tt + 1) * LANES, s * LANES:(s + 1) * LANES] = (
                blk_ref[pl.ds(off, LANES), :])

    half = lax.broadcasted_iota(jnp.int32, (SSM_GROUP, LANES), 1) // SSM_STATE
    for tt in range(SLABS_PER_TILE):
        for h in range(GROUPS_PER_BLOCK):
            own = half == (h % GROUPS_PER_TILE)
            r0 = tt * LANES + h * SSM_GROUP
            for d in range(2):
                c_re = cdup_ref[0, d, h * SSM_GROUP:(h + 1) * SSM_GROUP, :]
                c_im = cdup_ref[1, d, h * SSM_GROUP:(h + 1) * SSM_GROUP, :]
                p_re = pwt_ref[tt, 0, d, h:h + 1, :]
                p_im = pwt_ref[tt, 1, d, h:h + 1, :]
                col = (h // GROUPS_PER_TILE) * STATE_TILE + d * 2 * LANES
                wc_ref[r0:r0 + SSM_GROUP, col:col + LANES] = jnp.where(
                    own, c_re * p_re - c_im * p_im, 0.0).astype(BF16)
                wc_ref[r0:r0 + SSM_GROUP, col + LANES:col + 2 * LANES] = jnp.where(
                    own, -(c_re * p_im + c_im * p_re), 0.0).astype(BF16)

    y = (lax.dot_general(z_ref[...], wt_ref[...], _NT, preferred_element_type=F32)
         + lax.dot_general(h_ref[...], wc_ref[...], _NT, preferred_element_type=F32))
    d_skip = d_ref[...]
    for tt in range(SLABS_PER_TILE):
        yt = y[:, tt * LANES:(tt + 1) * LANES].reshape(N_CHUNKS, BATCH, LANES)
        y_ref[:, tt] = jax.nn.gelu(yt + d_skip * ue_ref[:, tt])


def _ssm_out(zflat, hstate, tapr, rep, cdup, pwt, u4, d_b):
    assert SSM_GROUP == BF16_ROWS
    return pl.pallas_call(
        _ssm_out_kernel,
        grid=(N_LANE_BLOCKS, CHUNK // SLABS_PER_TILE),
        in_specs=[
            pl.BlockSpec((None, CHUNK_ROWS, ROW_W), lambda g, n: (g, 0, 0)),
            pl.BlockSpec((None, CHUNK_ROWS, STATE_W), lambda g, n: (g, 0, 0)),
            pl.BlockSpec((None, N_LAGS * LANES, SSM_GROUP), lambda g, n: (g, 0, 0)),
            pl.BlockSpec((SSM_GROUP, LANES), lambda g, n: (0, 0)),
            pl.BlockSpec((None, 2, 2, LANES, LANES), lambda g, n: (g, 0, 0, 0, 0)),
            pl.BlockSpec((None, SLABS_PER_TILE, 2, 2, GROUPS_PER_BLOCK, LANES),
                         lambda g, n: (g, n, 0, 0, 0, 0)),
            pl.BlockSpec((N_CHUNKS, SLABS_PER_TILE, BATCH, LANES), lambda g, n: (0, n, 0, g)),
            pl.BlockSpec((None, SUBLANES, LANES), lambda g, n: (g, 0, 0)),
        ],
        out_specs=pl.BlockSpec((N_CHUNKS, SLABS_PER_TILE, BATCH, LANES),
                               lambda g, n: (0, n, 0, g)),
        out_shape=jax.ShapeDtypeStruct((N_CHUNKS, CHUNK, BATCH, D_SSM), F32),
        scratch_shapes=[pltpu.VMEM((N_LAGS * LANES, LANES), BF16),
                        pltpu.VMEM((OUT_TILE, ROW_W), BF16),
                        pltpu.VMEM((OUT_TILE, STATE_W), BF16)],
        compiler_params=pltpu.CompilerParams(
            dimension_semantics=("arbitrary", "arbitrary"),
            vmem_limit_bytes=VMEM_LIMIT),
        name="ssm_out",
    )(zflat, hstate, tapr, rep, cdup, pwt, u4, d_b)


def _glu_kernel(y_ref, g_ref, w_ref, b_ref, o_ref):
    y = y_ref[...]
    z = jnp.dot(y.astype(BF16), w_ref[...], preferred_element_type=F32) + b_ref[...]
    g = g_ref[...]
    o_ref[...] = (y * jax.nn.sigmoid(z) * (g * jax.nn.sigmoid(g))).astype(BF16)


def _glu(y2d, g2d, w_bf16, b_row):
    n_rows = BATCH * SEQ
    return pl.pallas_call(
        _glu_kernel,
        grid=(n_rows // TM_GLU,),
        in_specs=[
            pl.BlockSpec((TM_GLU, D_SSM), lambda i: (i, 0)),
            pl.BlockSpec((TM_GLU, D_SSM), lambda i: (i, 0)),
            pl.BlockSpec((D_SSM, D_SSM), lambda i: (0, 0)),
            pl.BlockSpec((1, D_SSM), lambda i: (0, 0)),
        ],
        out_specs=pl.BlockSpec((TM_GLU, D_SSM), lambda i: (i, 0)),
        out_shape=jax.ShapeDtypeStruct((n_rows, D_SSM), BF16),
        compiler_params=pltpu.CompilerParams(
            dimension_semantics=("parallel",), vmem_limit_bytes=VMEM_LIMIT),
        name="glu",
    )(y2d, g2d, w_bf16, b_row)


def _attn_kernel(lam_ref, q_ref, k_ref, v_ref, ga_ref, ng_ref, o_ref, *, post_scale):
    k = k_ref[...]
    v = v_ref[...]
    lam = lam_ref[0]
    lane = lax.broadcasted_iota(jnp.int32, (TQ, V_DIM), 1)
    zero = jnp.zeros((TQ, V_DIM), BF16)

    def scores(r):
        q = q_ref[r * TQ:(r + 1) * TQ, :]
        return (lax.dot_general(jnp.where(lane < HEAD_DIM, q, zero), k, _NT,
                                preferred_element_type=F32),
                lax.dot_general(jnp.where(lane >= HEAD_DIM, q, zero), k, _NT,
                                preferred_element_type=F32))

    def exp_sum(s):
        e = jnp.exp2(s - jnp.max(s, axis=1, keepdims=True))
        return e, jnp.sum(e, axis=1, keepdims=True)

    n_tiles = TQ_STEP // TQ
    s_next = scores(0)
    for r in range(n_tiles):
        s1, s2 = s_next
        if r + 1 < n_tiles:
            s_next = scores(r + 1)
        e1, l1 = exp_sum(s1)
        e2, l2 = exp_sum(s2)
        w = (e1 - (lam * l1 / l2) * e2).astype(BF16)
        o = jnp.dot(w, v, preferred_element_type=F32) / l1
        ms = jnp.mean(o * o, axis=1, keepdims=True)
        o = o * lax.rsqrt(ms + RMS_EPS) * ng_ref[...] * post_scale
        ga = ga_ref[r * TQ:(r + 1) * TQ, :]
        o_ref[r * TQ:(r + 1) * TQ, :] = (o * (ga * jax.nn.sigmoid(ga))).astype(BF16)


def _attention(lam, q, k, v, ga, ng_row, *, post_scale):
    nq = SEQ // TQ_STEP

    def q_map(b, h, i):
        return (b * nq + i, h)

    def kv_map(b, h, i):
        return (b, h)

    return pl.pallas_call(
        functools.partial(_attn_kernel, post_scale=post_scale),
        grid=(BATCH, N_HEADS, nq),
        in_specs=[
            pl.BlockSpec(memory_space=pltpu.SMEM),
            pl.BlockSpec((TQ_STEP, V_DIM), q_map),
            pl.BlockSpec((SEQ, V_DIM), kv_map),
            pl.BlockSpec((SEQ, V_DIM), kv_map),
            pl.BlockSpec((TQ_STEP, V_DIM), q_map),
            pl.BlockSpec((1, V_DIM), lambda b, h, i: (0, 0)),
        ],
        out_specs=pl.BlockSpec((TQ_STEP, V_DIM), q_map),
        out_shape=jax.ShapeDtypeStruct((BATCH * SEQ, D_ATTN), BF16),
        compiler_params=pltpu.CompilerParams(
            dimension_semantics=("parallel", "parallel", "arbitrary"),
            vmem_limit_bytes=VMEM_LIMIT),
        name="diff_attn",
    )(lam, q, k, v, ga, ng_row)


def _out_proj_kernel(ms_ref, ma_ref, w1_ref, w2_ref, x_ref, eg_ref, eb_ref, g_ref, b_ref,
                     o_ref, *, pre_ln):
    x = x_ref[...]
    if pre_ln:
        x = _layer_norm(x, eg_ref[...], eb_ref[...])
    out = (jnp.dot(ms_ref[...], w1_ref[...], preferred_element_type=F32)
           + jnp.dot(ma_ref[...], w2_ref[...], preferred_element_type=F32))
    o_ref[...] = _layer_norm(DEEPNORM_ALPHA * x + out, g_ref[...], b_ref[...])


def _out_proj(mix_ssm_tb, mix_attn, w1, w2, x2d, eg, eb, g, b, *, pre_ln):
    nt = SEQ // TM_OUT
    n_rows = BATCH * SEQ
    row = lambda i: (0, 0)
    return pl.pallas_call(
        functools.partial(_out_proj_kernel, pre_ln=pre_ln),
        grid=(n_rows // TM_OUT,),
        in_specs=[
            pl.BlockSpec((TM_OUT, D_SSM), lambda i: (i % nt, i // nt)),
            pl.BlockSpec((TM_OUT, D_ATTN), lambda i: (i, 0)),
            pl.BlockSpec((D_SSM, D_MODEL), row),
            pl.BlockSpec((D_ATTN, D_MODEL), row),
            pl.BlockSpec((TM_OUT, D_MODEL), lambda i: (i, 0)),
            pl.BlockSpec((1, D_MODEL), row),
            pl.BlockSpec((1, D_MODEL), row),
            pl.BlockSpec((1, D_MODEL), row),
            pl.BlockSpec((1, D_MODEL), row),
        ],
        out_specs=pl.BlockSpec((TM_OUT, D_MODEL), lambda i: (i, 0)),
        out_shape=jax.ShapeDtypeStruct((n_rows, D_MODEL), F32),
        compiler_params=pltpu.CompilerParams(
            dimension_semantics=("parallel",), vmem_limit_bytes=VMEM_LIMIT),
        name="out_proj",
    )(mix_ssm_tb, mix_attn, w1, w2, x2d, eg, eb, g, b)


def _rotary_tables():
    pos = jnp.arange(SEQ, dtype=F32)
    inv_freq = ROPE_THETA ** (-jnp.arange(0, ROT_DIM, 2, dtype=F32) / ROT_DIM)
    ang = pos[:, None] * inv_freq[None, :]
    cos = jnp.cos(ang)
    sin = jnp.sin(ang)
    ones = jnp.ones((SEQ, HEAD_DIM - ROT_DIM), F32)
    zeros = jnp.zeros((SEQ, HEAD_DIM - ROT_DIM), F32)
    z8 = jnp.zeros((SEQ, ROT_HALF), F32)
    reps = LANES // HEAD_DIM
    cos_t = jnp.tile(jnp.concatenate([cos, cos, ones], axis=1), (1, reps))
    sin_a = jnp.tile(jnp.concatenate([-sin, z8, zeros], axis=1), (1, reps))
    sin_b = jnp.tile(jnp.concatenate([z8, sin, zeros], axis=1), (1, reps))
    return cos_t, sin_a, sin_b


def _ssm_tables(lam_re, lam_im, log_step, b_re, b_im, c_re, c_im):
    G, P, C, L = N_SSM_GROUPS, SSM_STATE, SSM_GROUP, CHUNK
    GB, NT, GT = N_LANE_BLOCKS, N_STATE_TILES, GROUPS_PER_TILE
    hi = lax.Precision.HIGHEST
    step = jnp.exp(log_step)[..., None]
    zr = lam_re * step
    zi = lam_im * step
    kpow = jnp.arange(L + 1, dtype=F32)[:, None, None, None]
    mag = jnp.exp(kpow * zr[None])
    pw = jnp.stack([mag * jnp.cos(kpow * zi[None]),
                    mag * jnp.sin(kpow * zi[None])])
    nr = pw[0, 1] - 1.0
    ni = pw[1, 1]
    den = lam_re * lam_re + lam_im * lam_im
    coef_re = (nr * lam_re + ni * lam_im) / den
    coef_im = (ni * lam_re - nr * lam_im) / den
    bb = jnp.stack([coef_re[..., None] * b_re - coef_im[..., None] * b_im,
                    coef_re[..., None] * b_im + coef_im[..., None] * b_re])

    m_re = pw[0, :L, ..., None] * bb[0][None] - pw[1, :L, ..., None] * bb[1][None]
    m_im = pw[0, :L, ..., None] * bb[1][None] + pw[1, :L, ..., None] * bb[0][None]
    taps = (jnp.einsum('dgop,kdgpi->kdgio', c_re, m_re, precision=hi)
            - jnp.einsum('dgop,kdgpi->kdgio', c_im, m_im, precision=hi))
    tap_lag = jnp.concatenate([taps[:0:-1, 1], (taps[0, 0] + taps[0, 1])[None], taps[1:, 0]])
    tapr = tap_lag.transpose(0, 1, 3, 2).reshape(N_LAGS, GB, LANES, C)
    tapr = tapr.transpose(1, 0, 2, 3).reshape(GB, N_LAGS * LANES, C).astype(BF16)

    pwc = pw.reshape(2, L + 1, 2, GB, NT, GT, P).transpose(3, 4, 0, 2, 5, 6, 1)
    pwc = pwc.reshape(GB, NT, 2, 2, LANES, L + 1)
    bbx = bb.reshape(2, 2, GB, NT, GT * P, C).transpose(2, 3, 0, 1, 4, 5)
    bbx = jnp.tile(bbx, (1, 1, 1, 1, 1, GROUPS_PER_BLOCK))
    a16 = pw[:, L].reshape(2, 2, GB, NT, 1, LANES).transpose(2, 3, 0, 1, 4, 5)
    a16 = jnp.broadcast_to(a16, (GB, NT, 2, 2, SUBLANES, LANES))

    cc = jnp.stack([c_re, c_im]).reshape(2, 2, GB, LANES, P).transpose(2, 0, 1, 3, 4)
    cdup = jnp.concatenate([cc, cc], axis=-1)
    ar_l = jnp.arange(L)
    pt = jnp.stack([pw[:, ar_l + 1, 0], pw[:, L - ar_l, 1]], axis=2)
    pt = pt.reshape(2, L, 2, GB, GROUPS_PER_BLOCK, P).transpose(3, 1, 0, 2, 4, 5)
    pwt = jnp.concatenate([pt, pt], axis=-1)
    return tapr, pwc, bbx, a16, cdup, pwt


def kernel(x, ln_emb_g, ln_emb_b, w_in, ssm_lam_re, ssm_lam_im, ssm_log_step, ssm_b_re,
           ssm_b_im, ssm_c_re, ssm_c_im, ssm_d, w_glu, b_glu, lambda_q1, lambda_k1,
           lambda_q2, lambda_k2, attn_norm_g, w_out, ln_g, ln_b):
    assert x.shape == (BATCH, SEQ, D_MODEL) and x.dtype == F32
    cos_t, sin_a, sin_b = _rotary_tables()
    eg = ln_emb_g.reshape(1, D_MODEL)
    eb = ln_emb_b.reshape(1, D_MODEL)
    x2d = x.reshape(BATCH * SEQ, D_MODEL)
    rep = (jnp.arange(LANES)[None, :] % SSM_GROUP == jnp.arange(SSM_GROUP)[:, None]).astype(BF16)

    for l in range(DEPTH):
        pre_ln = l == 0
        u_tb, gs_tb, q, k, v, ga = _in_proj(
            x2d, eg, eb, w_in[l].astype(BF16), cos_t, sin_a, sin_b, pre_ln=pre_ln)

        tapr, pwc, bbx, a16, cdup, pwt = _ssm_tables(
            ssm_lam_re[l], ssm_lam_im[l], ssm_log_step[l], ssm_b_re[l], ssm_b_im[l],
            ssm_c_re[l], ssm_c_im[l])
        u4 = u_tb.reshape(N_CHUNKS, CHUNK, BATCH, D_SSM)
        zflat, hstate = _ssm_state(u4, pwc, bbx, a16)
        d_b = jnp.broadcast_to(ssm_d[l].reshape(N_LANE_BLOCKS, 1, LANES),
                               (N_LANE_BLOCKS, SUBLANES, LANES))
        y4 = _ssm_out(zflat, hstate, tapr, rep, cdup, pwt, u4, d_b)
        mix_ssm = _glu(y4.reshape(SEQ * BATCH, D_SSM), gs_tb.reshape(SEQ * BATCH, D_SSM),
                       w_glu[l].astype(BF16), b_glu[l].reshape(1, D_SSM))

        lambda_init = 0.8 - 0.6 * math.exp(-0.3 * l)
        lam = (jnp.exp(jnp.sum(lambda_q1[l] * lambda_k1[l]))
               - jnp.exp(jnp.sum(lambda_q2[l] * lambda_k2[l])) + lambda_init)
        mix_attn = _attention(lam.reshape(1).astype(F32), q, k, v, ga,
                              attn_norm_g[l].reshape(1, V_DIM), post_scale=1.0 - lambda_init)

        w_o = w_out[l].astype(BF16)
        x2d = _out_proj(mix_ssm.reshape(SEQ, BATCH * D_SSM), mix_attn, w_o[:D_SSM], w_o[D_SSM:],
                        x2d, eg, eb, ln_g[l].reshape(1, D_MODEL), ln_b[l].reshape(1, D_MODEL),
                        pre_ln=pre_ln)
    return x2d.reshape(BATCH, SEQ, D_MODEL)
```

```python
import functools
import math

import jax
import jax.numpy as jnp
from jax import lax
from jax.experimental import pallas as pl
from jax.experimental.pallas import tpu as pltpu

F32 = jnp.float32
BF16 = jnp.bfloat16

D_MODEL = 2048
BATCH = 8
SEQ = 2048
DEPTH = 2
D_SSM = 1024
D_ATTN = 1024
SSM_GROUP = 16
N_SSM_GROUPS = D_SSM // SSM_GROUP
SSM_STATE = 64
HEAD_DIM = 64
V_DIM = 2 * HEAD_DIM
N_HEADS = D_ATTN // V_DIM
ROT_DIM = HEAD_DIM // 4
ROT_HALF = ROT_DIM // 2
ROPE_THETA = 500000.0
LN_EPS = 1e-5
RMS_EPS = 1e-5
DEEPNORM_ALPHA = (2.0 * DEPTH) ** 0.25
PROJ_W = 1024

LANES = 128
SUBLANES = 8
BF16_ROWS = 16

CHUNK = 16
N_CHUNKS = SEQ // CHUNK
N_LAGS = 2 * CHUNK - 1
GROUPS_PER_BLOCK = LANES // SSM_GROUP
N_LANE_BLOCKS = D_SSM // LANES
ROW_W = CHUNK * LANES
GROUPS_PER_TILE = LANES // SSM_STATE
N_STATE_TILES = GROUPS_PER_BLOCK // GROUPS_PER_TILE
STATE_TILE = 2 * 2 * LANES
STATE_W = N_STATE_TILES * STATE_TILE
SLABS_PER_TILE = 4
OUT_TILE = SLABS_PER_TILE * LANES
CHUNK_ROWS = N_CHUNKS * BATCH

TM_PROJ = 512
PROJ_SLICE = 256
TM_GLU = 512
TM_OUT = 512
OUT_ROW_CHUNK = 256
TQ = 256
TQ_STEP = 1024
Q_SCALE = HEAD_DIM ** -0.5 * math.log2(math.e)

VMEM_LIMIT = 56 * 1024 * 1024

_NT = (((1,), (1,)), ((), ()))


def _layer_norm(x, g, b):
    mu = jnp.mean(x, axis=-1, keepdims=True)
    xc = x - mu
    var = jnp.mean(xc * xc, axis=-1, keepdims=True)
    return xc * lax.rsqrt(var + LN_EPS) * g + b


def _emb_ln_kernel(x_ref, g_ref, b_ref, o_ref):
    o_ref[...] = _layer_norm(x_ref[...], g_ref[...], b_ref[...]).astype(BF16)


def _emb_ln(x2d, eg, eb):
    n_rows = BATCH * SEQ
    return pl.pallas_call(
        _emb_ln_kernel,
        grid=(n_rows // TM_PROJ,),
        in_specs=[
            pl.BlockSpec((TM_PROJ, D_MODEL), lambda i: (i, 0)),
            pl.BlockSpec((1, D_MODEL), lambda i: (0, 0)),
            pl.BlockSpec((1, D_MODEL), lambda i: (0, 0)),
        ],
        out_specs=pl.BlockSpec((TM_PROJ, D_MODEL), lambda i: (i, 0)),
        out_shape=jax.ShapeDtypeStruct((n_rows, D_MODEL), BF16),
        compiler_params=pltpu.CompilerParams(
            dimension_semantics=("parallel",), vmem_limit_bytes=VMEM_LIMIT),
        name="emb_ln",
    )(x2d, eg, eb)


def _proj_kernel(x_ref, w_ref, *refs, kind):
    if kind == "qk":
        cos_ref, sa_ref, sb_ref, oa_ref, ob_ref = refs
    else:
        oa_ref, ob_ref = refs
    x = x_ref[...]
    slices_per_piece = PROJ_W // PROJ_SLICE
    for n in range(2 * slices_per_piece):
        acc = jnp.dot(x, w_ref[:, n * PROJ_SLICE:(n + 1) * PROJ_SLICE],
                      preferred_element_type=F32)
        first = n < slices_per_piece
        o_ref = oa_ref if first else ob_ref
        col = (n % slices_per_piece) * PROJ_SLICE
        if kind == "qk":
            for cb in range(PROJ_SLICE // LANES):
                t = acc[:, cb * LANES:(cb + 1) * LANES]
                r = (t * cos_ref[...] + pltpu.roll(t, LANES - ROT_HALF, 1) * sa_ref[...]
                     + pltpu.roll(t, ROT_HALF, 1) * sb_ref[...])
                if first:
                    r = r * Q_SCALE
                o_ref[:, col + cb * LANES:col + (cb + 1) * LANES] = r.astype(BF16)
        else:
            o_ref[:, col:col + PROJ_SLICE] = acc.astype(o_ref.dtype)


def _proj_pair(xb, w_bf16, layer, pair, kind, tables=()):
    nt = SEQ // TM_PROJ
    n_rows = BATCH * SEQ
    nat_map = lambda i: (i, 0)
    if kind == "ssm":
        shape, o_map, dtypes = (SEQ, BATCH * PROJ_W), (lambda i: (i % nt, i // nt)), (F32, F32)
    elif kind == "qk":
        shape, o_map, dtypes = (n_rows, PROJ_W), nat_map, (BF16, BF16)
    else:
        shape, o_map, dtypes = (n_rows, PROJ_W), nat_map, (BF16, F32)
    table_specs = [pl.BlockSpec((TM_PROJ, LANES), lambda i: (i % nt, 0)) for _ in tables]
    return pl.pallas_call(
        functools.partial(_proj_kernel, kind=kind),
        grid=(n_rows // TM_PROJ,),
        in_specs=[
            pl.BlockSpec((TM_PROJ, D_MODEL), nat_map),
            pl.BlockSpec((None, D_MODEL, 2 * PROJ_W), lambda i: (layer, 0, pair)),
        ] + table_specs,
        out_specs=[pl.BlockSpec((TM_PROJ, PROJ_W), o_map)] * 2,
        out_shape=[jax.ShapeDtypeStruct(shape, dt) for dt in dtypes],
        compiler_params=pltpu.CompilerParams(
            dimension_semantics=("parallel",), vmem_limit_bytes=VMEM_LIMIT),
        name="proj_" + kind,
    )(xb, w_bf16, *tables)


def _ssm_state_kernel(u_ref, pwc_ref, bbx_ref, a_ref, z_ref, h_ref, bt_ref, zs_ref, hs_ref):
    n = pl.program_id(1)

    @pl.when(n == 0)
    def _():
        for s in range(CHUNK):
            z_ref[:, s * LANES:(s + 1) * LANES] = (
                u_ref[:, s].reshape(CHUNK_ROWS, LANES).astype(BF16))

    row = lax.broadcasted_iota(jnp.int32, (LANES, LANES), 0)
    lane = lax.broadcasted_iota(jnp.int32, (LANES, LANES), 1)
    own = (lane // SSM_GROUP) == (n * GROUPS_PER_TILE + row // SSM_STATE)
    for d in range(2):
        b_re = jnp.where(own, bbx_ref[0, d], 0.0)
        b_im = jnp.where(own, bbx_ref[1, d], 0.0)
        for s in range(CHUNK):
            e = CHUNK - 1 - s if d == 0 else s
            p_re = pwc_ref[0, d, :, e:e + 1]
            p_im = pwc_ref[1, d, :, e:e + 1]
            r0 = d * 2 * LANES
            bt_ref[r0:r0 + LANES, s * LANES:(s + 1) * LANES] = (
                p_re * b_re - p_im * b_im).astype(BF16)
            bt_ref[r0 + LANES:r0 + 2 * LANES, s * LANES:(s + 1) * LANES] = (
                p_re * b_im + p_im * b_re).astype(BF16)

    zs_ref[...] = lax.dot_general(z_ref[...], bt_ref[...], _NT, preferred_element_type=F32)

    fa_re, fa_im = a_ref[0, 0], a_ref[1, 0]
    ba_re, ba_im = a_ref[0, 1], a_ref[1, 1]

    def body(c, carry):
        f_re, f_im, b_re, b_im = carry
        rf = pl.multiple_of(c * BATCH, SUBLANES)
        rb = pl.multiple_of((N_CHUNKS - 1 - c) * BATCH, SUBLANES)
        hs_ref[pl.ds(rf, BATCH), 0:LANES] = f_re
        hs_ref[pl.ds(rf, BATCH), LANES:2 * LANES] = f_im
        hs_ref[pl.ds(rb, BATCH), 2 * LANES:3 * LANES] = b_re
        hs_ref[pl.ds(rb, BATCH), 3 * LANES:4 * LANES] = b_im
        zf_re = zs_ref[pl.ds(rf, BATCH), 0:LANES]
        zf_im = zs_ref[pl.ds(rf, BATCH), LANES:2 * LANES]
        zb_re = zs_ref[pl.ds(rb, BATCH), 2 * LANES:3 * LANES]
        zb_im = zs_ref[pl.ds(rb, BATCH), 3 * LANES:4 * LANES]
        return (fa_re * f_re - fa_im * f_im + zf_re, fa_re * f_im + fa_im * f_re + zf_im,
                ba_re * b_re - ba_im * b_im + zb_re, ba_re * b_im + ba_im * b_re + zb_im)

    zero = jnp.zeros((BATCH, LANES), F32)
    lax.fori_loop(0, N_CHUNKS, body, (zero, zero, zero, zero))
    h_ref[...] = hs_ref[...].astype(BF16)


def _ssm_state(u4, pwc, bbx, a16):
    tile_map = lambda g, n: (g, n, 0, 0, 0, 0)
    return pl.pallas_call(
        _ssm_state_kernel,
        grid=(N_LANE_BLOCKS, N_STATE_TILES),
        in_specs=[
            pl.BlockSpec((N_CHUNKS, CHUNK, BATCH, LANES), lambda g, n: (0, 0, 0, g)),
            pl.BlockSpec((None, None, 2, 2, LANES, CHUNK + 1), tile_map),
            pl.BlockSpec((None, None, 2, 2, LANES, LANES), tile_map),
            pl.BlockSpec((None, None, 2, 2, SUBLANES, LANES), tile_map),
        ],
        out_specs=[
            pl.BlockSpec((None, CHUNK_ROWS, ROW_W), lambda g, n: (g, 0, 0)),
            pl.BlockSpec((None, CHUNK_ROWS, STATE_TILE), lambda g, n: (g, 0, n)),
        ],
        out_shape=[
            jax.ShapeDtypeStruct((N_LANE_BLOCKS, CHUNK_ROWS, ROW_W), BF16),
            jax.ShapeDtypeStruct((N_LANE_BLOCKS, CHUNK_ROWS, STATE_W), BF16),
        ],
        scratch_shapes=[pltpu.VMEM((STATE_TILE, ROW_W), BF16),
                        pltpu.VMEM((CHUNK_ROWS, STATE_TILE), F32),
                        pltpu.VMEM((CHUNK_ROWS, STATE_TILE), F32)],
        compiler_params=pltpu.CompilerParams(
            dimension_semantics=("arbitrary", "arbitrary"),
            vmem_limit_bytes=VMEM_LIMIT),
        name="ssm_state",
    )(u4, pwc, bbx, a16)


def _ssm_out_kernel(z_ref, h_ref, tap_ref, rep_ref, cdup_ref, pwt_ref, ue_ref, d_ref, y_ref,
                    blk_ref, wt_ref, wc_ref):
    g = pl.program_id(0)
    n = pl.program_id(1)

    @pl.when(n == 0)
    def _():
        x = jnp.dot(tap_ref[...], rep_ref[...], preferred_element_type=F32)
        row = lax.broadcasted_iota(jnp.int32, x.shape, 0)
        lane = lax.broadcasted_iota(jnp.int32, x.shape, 1)
        own = ((row % LANES) // SSM_GROUP) == (lane // SSM_GROUP)
        blk_ref[...] = jnp.where(own, x, 0.0).astype(BF16)

    @pl.when((g == 0) & (n == 0))
    def _():
        wc_ref[...] = jnp.zeros_like(wc_ref)

    for tt in range(SLABS_PER_TILE):
        t = n * SLABS_PER_TILE + tt
        for s in range(CHUNK):
            off = pl.multiple_of((t - s + CHUNK - 1) * LANES, LANES)
            wt_ref[tt * LANES:(tt + 1) * LANES, s * LANES:(s + 1) * LANES] = (
                blk_ref[pl.ds(off, LANES), :])

    half = lax.broadcasted_iota(jnp.int32, (SSM_GROUP, LANES), 1) // SSM_STATE
    for tt in range(SLABS_PER_TILE):
        for h in range(GROUPS_PER_BLOCK):
            own = half == (h % GROUPS_PER_TILE)
            r0 = tt * LANES + h * SSM_GROUP
            for d in range(2):
                c_re = cdup_ref[0, d, h * SSM_GROUP:(h + 1) * SSM_GROUP, :]
                c_im = cdup_ref[1, d, h * SSM_GROUP:(h + 1) * SSM_GROUP, :]
                p_re = pwt_ref[tt, 0, d, h:h + 1, :]
                p_im = pwt_ref[tt, 1, d, h:h + 1, :]
                col = (h // GROUPS_PER_TILE) * STATE_TILE + d * 2 * LANES
                wc_ref[r0:r0 + SSM_GROUP, col:col + LANES] = jnp.where(
                    own, c_re * p_re - c_im * p_im, 0.0).astype(BF16)
                wc_ref[r0:r0 + SSM_GROUP, col + LANES:col + 2 * LANES] = jnp.where(
                    own, -(c_re * p_im + c_im * p_re), 0.0).astype(BF16)

    y = (lax.dot_general(z_ref[...], wt_ref[...], _NT, preferred_element_type=F32)
         + lax.dot_general(h_ref[...], wc_ref[...], _NT, preferred_element_type=F32))
    d_skip = d_ref[...]
    for tt in range(SLABS_PER_TILE):
        yt = y[:, tt * LANES:(tt + 1) * LANES].reshape(N_CHUNKS, BATCH, LANES)
        y_ref[:, tt] = jax.nn.gelu(yt + d_skip * ue_ref[:, tt])


def _ssm_out(zflat, hstate, tapr, rep, cdup, pwt, u4, d_b):
    assert SSM_GROUP == BF16_ROWS
    return pl.pallas_call(
        _ssm_out_kernel,
        grid=(N_LANE_BLOCKS, CHUNK // SLABS_PER_TILE),
        in_specs=[
            pl.BlockSpec((None, CHUNK_ROWS, ROW_W), lambda g, n: (g, 0, 0)),
            pl.BlockSpec((None, CHUNK_ROWS, STATE_W), lambda g, n: (g, 0, 0)),
            pl.BlockSpec((None, N_LAGS * LANES, SSM_GROUP), lambda g, n: (g, 0, 0)),
            pl.BlockSpec((SSM_GROUP, LANES), lambda g, n: (0, 0)),
            pl.BlockSpec((None, 2, 2, LANES, LANES), lambda g, n: (g, 0, 0, 0, 0)),
            pl.BlockSpec((None, SLABS_PER_TILE, 2, 2, GROUPS_PER_BLOCK, LANES),
                         lambda g, n: (g, n, 0, 0, 0, 0)),
            pl.BlockSpec((N_CHUNKS, SLABS_PER_TILE, BATCH, LANES), lambda g, n: (0, n, 0, g)),
            pl.BlockSpec((None, SUBLANES, LANES), lambda g, n: (g, 0, 0)),
        ],
        out_specs=pl.BlockSpec((N_CHUNKS, SLABS_PER_TILE, BATCH, LANES),
                               lambda g, n: (0, n, 0, g)),
        out_shape=jax.ShapeDtypeStruct((N_CHUNKS, CHUNK, BATCH, D_SSM), F32),
        scratch_shapes=[pltpu.VMEM((N_LAGS * LANES, LANES), BF16),
                        pltpu.VMEM((OUT_TILE, ROW_W), BF16),
                        pltpu.VMEM((OUT_TILE, STATE_W), BF16)],
        compiler_params=pltpu.CompilerParams(
            dimension_semantics=("arbitrary", "arbitrary"),
            vmem_limit_bytes=VMEM_LIMIT),
        name="ssm_out",
    )(zflat, hstate, tapr, rep, cdup, pwt, u4, d_b)


def _glu_kernel(y_ref, g_ref, w_ref, b_ref, o_ref):
    y = y_ref[...]
    z = jnp.dot(y.astype(BF16), w_ref[...], preferred_element_type=F32) + b_ref[...]
    g = g_ref[...]
    o_ref[...] = (y * jax.nn.sigmoid(z) * (g * jax.nn.sigmoid(g))).astype(BF16)


def _glu(y2d, g2d, w_bf16, b_row):
    n_rows = BATCH * SEQ
    return pl.pallas_call(
        _glu_kernel,
        grid=(n_rows // TM_GLU,),
        in_specs=[
            pl.BlockSpec((TM_GLU, D_SSM), lambda i: (i, 0)),
            pl.BlockSpec((TM_GLU, D_SSM), lambda i: (i, 0)),
            pl.BlockSpec((D_SSM, D_SSM), lambda i: (0, 0)),
            pl.BlockSpec((1, D_SSM), lambda i: (0, 0)),
        ],
        out_specs=pl.BlockSpec((TM_GLU, D_SSM), lambda i: (i, 0)),
        out_shape=jax.ShapeDtypeStruct((n_rows, D_SSM), BF16),
        compiler_params=pltpu.CompilerParams(
            dimension_semantics=("parallel",), vmem_limit_bytes=VMEM_LIMIT),
        name="glu",
    )(y2d, g2d, w_bf16, b_row)


def _attn_kernel(lam_ref, q_ref, k_ref, v_ref, ga_ref, ng_ref, o_ref, *, post_scale):
    k = k_ref[...]
    v = v_ref[...]
    lam = lam_ref[0]
    lane = lax.broadcasted_iota(jnp.int32, (TQ, V_DIM), 1)
    zero = jnp.zeros((TQ, V_DIM), BF16)

    def scores(r):
        q = q_ref[r * TQ:(r + 1) * TQ, :]
        return (lax.dot_general(jnp.where(lane < HEAD_DIM, q, zero), k, _NT,
                                preferred_element_type=F32),
                lax.dot_general(jnp.where(lane >= HEAD_DIM, q, zero), k, _NT,
                                preferred_element_type=F32))

    def exp_sum(s):
        e = jnp.exp2(s - jnp.max(s, axis=1, keepdims=True))
        return e, jnp.sum(e, axis=1, keepdims=True)

    n_tiles = TQ_STEP // TQ
    s_next = scores(0)
    for r in range(n_tiles):
        s1, s2 = s_next
        if r + 1 < n_tiles:
            s_next = scores(r + 1)
        e1, l1 = exp_sum(s1)
        e2, l2 = exp_sum(s2)
        w = (e1 - (lam * l1 / l2) * e2).astype(BF16)
        o = jnp.dot(w, v, preferred_element_type=F32) / l1
        ms = jnp.mean(o * o, axis=1, keepdims=True)
        o = o * lax.rsqrt(ms + RMS_EPS) * ng_ref[...] * post_scale
        ga = ga_ref[r * TQ:(r + 1) * TQ, :]
        o_ref[r * TQ:(r + 1) * TQ, :] = (o * (ga * jax.nn.sigmoid(ga))).astype(BF16)


def _attention(lam, q, k, v, ga, ng_row, *, post_scale):
    nq = SEQ // TQ_STEP

    def q_map(b, h, i):
        return (b * nq + i, h)

    def kv_map(b, h, i):
        return (b, h)

    return pl.pallas_call(
        functools.partial(_attn_kernel, post_scale=post_scale),
        grid=(BATCH, N_HEADS, nq),
        in_specs=[
            pl.BlockSpec(memory_space=pltpu.SMEM),
            pl.BlockSpec((TQ_STEP, V_DIM), q_map),
            pl.BlockSpec((SEQ, V_DIM), kv_map),
            pl.BlockSpec((SEQ, V_DIM), kv_map),
            pl.BlockSpec((TQ_STEP, V_DIM), q_map),
            pl.BlockSpec((1, V_DIM), lambda b, h, i: (0, 0)),
        ],
        out_specs=pl.BlockSpec((TQ_STEP, V_DIM), q_map),
        out_shape=jax.ShapeDtypeStruct((BATCH * SEQ, D_ATTN), BF16),
        compiler_params=pltpu.CompilerParams(
            dimension_semantics=("parallel", "parallel", "arbitrary"),
            vmem_limit_bytes=VMEM_LIMIT),
        name="diff_attn",
    )(lam, q, k, v, ga, ng_row)


def _out_proj_kernel(ms_ref, ma_ref, w1_ref, w2_ref, x_ref, eg_ref, eb_ref, g_ref, b_ref,
                     o_ref, *maybe_ob_ref, pre_ln):
    for c in range(TM_OUT // OUT_ROW_CHUNK):
        rows = slice(c * OUT_ROW_CHUNK, (c + 1) * OUT_ROW_CHUNK)
        x = x_ref[rows, :]
        if pre_ln:
            x = _layer_norm(x, eg_ref[...], eb_ref[...])
        out = (jnp.dot(ms_ref[rows, :], w1_ref[...], preferred_element_type=F32)
               + jnp.dot(ma_ref[rows, :], w2_ref[...], preferred_element_type=F32))
        y = _layer_norm(DEEPNORM_ALPHA * x + out, g_ref[...], b_ref[...])
        o_ref[rows, :] = y
        for ob_ref in maybe_ob_ref:
            ob_ref[rows, :] = y.astype(BF16)


def _out_proj(mix_ssm_tb, mix_attn, w_bf16, layer, x2d, eg, eb, g, b, *, pre_ln, emit_bf16):
    nt = SEQ // TM_OUT
    n_rows = BATCH * SEQ
    row = lambda i: (0, 0)
    nat = pl.BlockSpec((TM_OUT, D_MODEL), lambda i: (i, 0))
    out_specs = [nat]
    out_shape = [jax.ShapeDtypeStruct((n_rows, D_MODEL), F32)]
    if emit_bf16:
        out_specs.append(nat)
        out_shape.append(jax.ShapeDtypeStruct((n_rows, D_MODEL), BF16))
    return pl.pallas_call(
        functools.partial(_out_proj_kernel, pre_ln=pre_ln),
        grid=(n_rows // TM_OUT,),
        in_specs=[
            pl.BlockSpec((TM_OUT, D_SSM), lambda i: (i % nt, i // nt)),
            pl.BlockSpec((TM_OUT, D_ATTN), lambda i: (i, 0)),
            pl.BlockSpec((None, D_SSM, D_MODEL), lambda i: (layer, 0, 0)),
            pl.BlockSpec((None, D_ATTN, D_MODEL), lambda i: (layer, 1, 0)),
            nat,
            pl.BlockSpec((1, D_MODEL), row),
            pl.BlockSpec((1, D_MODEL), row),
            pl.BlockSpec((1, D_MODEL), row),
            pl.BlockSpec((1, D_MODEL), row),
        ],
        out_specs=out_specs,
        out_shape=out_shape,
        compiler_params=pltpu.CompilerParams(
            dimension_semantics=("parallel",), vmem_limit_bytes=VMEM_LIMIT),
        name="out_proj",
    )(mix_ssm_tb, mix_attn, w_bf16, w_bf16, x2d, eg, eb, g, b)


def _rotary_tables():
    pos = jnp.arange(SEQ, dtype=F32)
    inv_freq = ROPE_THETA ** (-jnp.arange(0, ROT_DIM, 2, dtype=F32) / ROT_DIM)
    ang = pos[:, None] * inv_freq[None, :]
    cos = jnp.cos(ang)
    sin = jnp.sin(ang)
    ones = jnp.ones((SEQ, HEAD_DIM - ROT_DIM), F32)
    zeros = jnp.zeros((SEQ, HEAD_DIM - ROT_DIM), F32)
    z8 = jnp.zeros((SEQ, ROT_HALF), F32)
    reps = LANES // HEAD_DIM
    cos_t = jnp.tile(jnp.concatenate([cos, cos, ones], axis=1), (1, reps))
    sin_a = jnp.tile(jnp.concatenate([-sin, z8, zeros], axis=1), (1, reps))
    sin_b = jnp.tile(jnp.concatenate([z8, sin, zeros], axis=1), (1, reps))
    return cos_t, sin_a, sin_b


def _ssm_tables(lam_re, lam_im, log_step, b_re, b_im, c_re, c_im):
    G, P, C, L = N_SSM_GROUPS, SSM_STATE, SSM_GROUP, CHUNK
    GB, NT, GT = N_LANE_BLOCKS, N_STATE_TILES, GROUPS_PER_TILE
    hi = lax.Precision.HIGHEST
    step = jnp.exp(log_step)[..., None]
    zr = lam_re * step
    zi = lam_im * step
    kpow = jnp.arange(L + 1, dtype=F32)[:, None, None, None]
    mag = jnp.exp(kpow * zr[None])
    pw = jnp.stack([mag * jnp.cos(kpow * zi[None]),
                    mag * jnp.sin(kpow * zi[None])])
    nr = pw[0, 1] - 1.0
    ni = pw[1, 1]
    den = lam_re * lam_re + lam_im * lam_im
    coef_re = (nr * lam_re + ni * lam_im) / den
    coef_im = (ni * lam_re - nr * lam_im) / den
    bb = jnp.stack([coef_re[..., None] * b_re - coef_im[..., None] * b_im,
                    coef_re[..., None] * b_im + coef_im[..., None] * b_re])

    m_re = pw[0, :L, ..., None] * bb[0][None] - pw[1, :L, ..., None] * bb[1][None]
    m_im = pw[0, :L, ..., None] * bb[1][None] + pw[1, :L, ..., None] * bb[0][None]
    taps = (jnp.einsum('dgop,kdgpi->kdgio', c_re, m_re, precision=hi)
            - jnp.einsum('dgop,kdgpi->kdgio', c_im, m_im, precision=hi))
    tap_lag = jnp.concatenate([taps[:0:-1, 1], (taps[0, 0] + taps[0, 1])[None], taps[1:, 0]])
    tapr = tap_lag.transpose(0, 1, 3, 2).reshape(N_LAGS, GB, LANES, C)
    tapr = tapr.transpose(1, 0, 2, 3).reshape(GB, N_LAGS * LANES, C).astype(BF16)

    pwc = pw.reshape(2, L + 1, 2, GB, NT, GT, P).transpose(3, 4, 0, 2, 5, 6, 1)
    pwc = pwc.reshape(GB, NT, 2, 2, LANES, L + 1)
    bbx = bb.reshape(2, 2, GB, NT, GT * P, C).transpose(2, 3, 0, 1, 4, 5)
    bbx = jnp.tile(bbx, (1, 1, 1, 1, 1, GROUPS_PER_BLOCK))
    a16 = pw[:, L].reshape(2, 2, GB, NT, 1, LANES).transpose(2, 3, 0, 1, 4, 5)
    a16 = jnp.broadcast_to(a16, (GB, NT, 2, 2, SUBLANES, LANES))

    cc = jnp.stack([c_re, c_im]).reshape(2, 2, GB, LANES, P).transpose(2, 0, 1, 3, 4)
    cdup = jnp.concatenate([cc, cc], axis=-1)
    ar_l = jnp.arange(L)
    pt = jnp.stack([pw[:, ar_l + 1, 0], pw[:, L - ar_l, 1]], axis=2)
    pt = pt.reshape(2, L, 2, GB, GROUPS_PER_BLOCK, P).transpose(3, 1, 0, 2, 4, 5)
    pwt = jnp.concatenate([pt, pt], axis=-1)
    return tapr, pwc, bbx, a16, cdup, pwt


def kernel(x, ln_emb_g, ln_emb_b, w_in, ssm_lam_re, ssm_lam_im, ssm_log_step, ssm_b_re,
           ssm_b_im, ssm_c_re, ssm_c_im, ssm_d, w_glu, b_glu, lambda_q1, lambda_k1,
           lambda_q2, lambda_k2, attn_norm_g, w_out, ln_g, ln_b):
    assert x.shape == (BATCH, SEQ, D_MODEL) and x.dtype == F32
    cos_t, sin_a, sin_b = _rotary_tables()
    eg = ln_emb_g.reshape(1, D_MODEL)
    eb = ln_emb_b.reshape(1, D_MODEL)
    x2d = x.reshape(BATCH * SEQ, D_MODEL)
    rep = (jnp.arange(LANES)[None, :] % SSM_GROUP == jnp.arange(SSM_GROUP)[:, None]).astype(BF16)

    w_in_bf16 = w_in.astype(BF16)
    w_out_bf16 = w_out.astype(BF16)
    xb = _emb_ln(x2d, eg, eb)
    for l in range(DEPTH):
        pre_ln = l == 0
        u_tb, gs_tb = _proj_pair(xb, w_in_bf16, l, 0, "ssm")
        q, k = _proj_pair(xb, w_in_bf16, l, 1, "qk", (cos_t, sin_a, sin_b))
        v, ga = _proj_pair(xb, w_in_bf16, l, 2, "vg")

        tapr, pwc, bbx, a16, cdup, pwt = _ssm_tables(
            ssm_lam_re[l], ssm_lam_im[l], ssm_log_step[l], ssm_b_re[l], ssm_b_im[l],
            ssm_c_re[l], ssm_c_im[l])
        u4 = u_tb.reshape(N_CHUNKS, CHUNK, BATCH, D_SSM)
        zflat, hstate = _ssm_state(u4, pwc, bbx, a16)
        d_b = jnp.broadcast_to(ssm_d[l].reshape(N_LANE_BLOCKS, 1, LANES),
                               (N_LANE_BLOCKS, SUBLANES, LANES))
        y4 = _ssm_out(zflat, hstate, tapr, rep, cdup, pwt, u4, d_b)
        mix_ssm = _glu(y4.reshape(SEQ * BATCH, D_SSM), gs_tb.reshape(SEQ * BATCH, D_SSM),
                       w_glu[l].astype(BF16), b_glu[l].reshape(1, D_SSM))

        lambda_init = 0.8 - 0.6 * math.exp(-0.3 * l)
        lam = (jnp.exp(jnp.sum(lambda_q1[l] * lambda_k1[l]))
               - jnp.exp(jnp.sum(lambda_q2[l] * lambda_k2[l])) + lambda_init)
        mix_attn = _attention(lam.reshape(1).astype(F32), q, k, v, ga,
                              attn_norm_g[l].reshape(1, V_DIM), post_scale=1.0 - lambda_init)

        outs = _out_proj(mix_ssm.reshape(SEQ, BATCH * D_SSM), mix_attn, w_out_bf16, l,
                         x2d, eg, eb, ln_g[l].reshape(1, D_MODEL), ln_b[l].reshape(1, D_MODEL),
                         pre_ln=pre_ln, emit_bf16=l + 1 < DEPTH)
        x2d = outs[0]
        if l + 1 < DEPTH:
            xb = outs[1]
    return x2d.reshape(BATCH, SEQ, D_MODEL)
```

```python
import functools
import math

import jax
import jax.numpy as jnp
from jax import lax
from jax.experimental import pallas as pl
from jax.experimental.pallas import tpu as pltpu

F32 = jnp.float32
BF16 = jnp.bfloat16

D_MODEL = 2048
BATCH = 8
SEQ = 2048
DEPTH = 2
D_SSM = 1024
D_ATTN = 1024
SSM_GROUP = 16
N_SSM_GROUPS = D_SSM // SSM_GROUP
SSM_STATE = 64
HEAD_DIM = 64
V_DIM = 2 * HEAD_DIM
N_HEADS = D_ATTN // V_DIM
ROT_DIM = HEAD_DIM // 4
ROT_HALF = ROT_DIM // 2
ROPE_THETA = 500000.0
LN_EPS = 1e-5
RMS_EPS = 1e-5
DEEPNORM_ALPHA = (2.0 * DEPTH) ** 0.25
PROJ_W = 1024

LANES = 128
SUBLANES = 8
BF16_ROWS = 16

CHUNK = 16
N_CHUNKS = SEQ // CHUNK
N_LAGS = 2 * CHUNK - 1
GROUPS_PER_BLOCK = LANES // SSM_GROUP
N_LANE_BLOCKS = D_SSM // LANES
ROW_W = CHUNK * LANES
GROUPS_PER_TILE = LANES // SSM_STATE
N_STATE_TILES = GROUPS_PER_BLOCK // GROUPS_PER_TILE
STATE_TILE = 2 * 2 * LANES
STATE_W = N_STATE_TILES * STATE_TILE
SLABS_PER_TILE = 4
OUT_TILE = SLABS_PER_TILE * LANES
CHUNK_ROWS = N_CHUNKS * BATCH

TM_PROJ = 512
PROJ_SLICE = 256
TM_GLU = 512
TM_OUT = 512
OUT_ROW_CHUNK = 256
TQ = 256
TQ_STEP = 1024
Q_SCALE = HEAD_DIM ** -0.5 * math.log2(math.e)

VMEM_LIMIT = 56 * 1024 * 1024

_NT = (((1,), (1,)), ((), ()))


def _layer_norm(x, g, b):
    mu = jnp.mean(x, axis=-1, keepdims=True)
    xc = x - mu
    var = jnp.mean(xc * xc, axis=-1, keepdims=True)
    return xc * lax.rsqrt(var + LN_EPS) * g + b


def _emb_ln_kernel(x_ref, g_ref, b_ref, o_ref):
    o_ref[...] = _layer_norm(x_ref[...], g_ref[...], b_ref[...]).astype(BF16)


def _emb_ln(x2d, eg, eb):
    n_rows = BATCH * SEQ
    return pl.pallas_call(
        _emb_ln_kernel,
        grid=(n_rows // TM_PROJ,),
        in_specs=[
            pl.BlockSpec((TM_PROJ, D_MODEL), lambda i: (i, 0)),
            pl.BlockSpec((1, D_MODEL), lambda i: (0, 0)),
            pl.BlockSpec((1, D_MODEL), lambda i: (0, 0)),
        ],
        out_specs=pl.BlockSpec((TM_PROJ, D_MODEL), lambda i: (i, 0)),
        out_shape=jax.ShapeDtypeStruct((n_rows, D_MODEL), BF16),
        compiler_params=pltpu.CompilerParams(
            dimension_semantics=("parallel",), vmem_limit_bytes=VMEM_LIMIT),
        name="emb_ln",
    )(x2d, eg, eb)


def _proj_kernel(x_ref, w_ref, *refs, kind):
    if kind == "qk":
        cos_ref, sa_ref, sb_ref, oa_ref, ob_ref = refs
    else:
        oa_ref, ob_ref = refs
    x = x_ref[...].reshape(TM_PROJ, D_MODEL)
    slices_per_piece = PROJ_W // PROJ_SLICE
    for n in range(2 * slices_per_piece):
        acc = jnp.dot(x, w_ref[:, n * PROJ_SLICE:(n + 1) * PROJ_SLICE],
                      preferred_element_type=F32)
        first = n < slices_per_piece
        o_ref = oa_ref if first else ob_ref
        col = (n % slices_per_piece) * PROJ_SLICE
        if kind == "ssm":
            t_rows = TM_PROJ // BATCH
            for cb in range(PROJ_SLICE // LANES):
                for b in range(BATCH):
                    o_ref[col // LANES + cb, pl.ds(b, t_rows, stride=BATCH), :] = (
                        acc[b * t_rows:(b + 1) * t_rows, cb * LANES:(cb + 1) * LANES])
        elif kind == "qk":
            for cb in range(PROJ_SLICE // LANES):
                t = acc[:, cb * LANES:(cb + 1) * LANES]
                r = (t * cos_ref[...] + pltpu.roll(t, LANES - ROT_HALF, 1) * sa_ref[...]
                     + pltpu.roll(t, ROT_HALF, 1) * sb_ref[...])
                if first:
                    r = r * Q_SCALE
                o_ref[:, col + cb * LANES:col + (cb + 1) * LANES] = r.astype(BF16)
        else:
            o_ref[:, col:col + PROJ_SLICE] = acc.astype(o_ref.dtype)


def _proj_pair(xb, w_bf16, layer, pair, kind, tables=()):
    nt = SEQ // TM_PROJ
    n_rows = BATCH * SEQ
    nat_map = lambda i: (i, 0)
    if kind == "ssm":
        x_arg = xb.reshape(BATCH, SEQ, D_MODEL)
        x_spec = pl.BlockSpec((BATCH, TM_PROJ // BATCH, D_MODEL), lambda i: (0, i, 0))
        o_spec = pl.BlockSpec((N_LANE_BLOCKS, TM_PROJ, LANES), lambda i: (0, i, 0))
        shape, dtypes = (N_LANE_BLOCKS, n_rows, LANES), (F32, F32)
    else:
        x_arg = xb
        x_spec = pl.BlockSpec((TM_PROJ, D_MODEL), nat_map)
        o_spec = pl.BlockSpec((TM_PROJ, PROJ_W), nat_map)
        shape, dtypes = (n_rows, PROJ_W), ((BF16, BF16) if kind == "qk" else (BF16, F32))
    table_specs = [pl.BlockSpec((TM_PROJ, LANES), lambda i: (i % nt, 0)) for _ in tables]
    return pl.pallas_call(
        functools.partial(_proj_kernel, kind=kind),
        grid=(n_rows // TM_PROJ,),
        in_specs=[
            x_spec,
            pl.BlockSpec((None, D_MODEL, 2 * PROJ_W), lambda i: (layer, 0, pair)),
        ] + table_specs,
        out_specs=[o_spec] * 2,
        out_shape=[jax.ShapeDtypeStruct(shape, dt) for dt in dtypes],
        compiler_params=pltpu.CompilerParams(
            dimension_semantics=("parallel",), vmem_limit_bytes=VMEM_LIMIT),
        name="proj_" + kind,
    )(x_arg, w_bf16, *tables)


def _ssm_state_kernel(u_ref, pwc_ref, bbx_ref, a_ref, z_ref, h_ref, bt_ref, zs_ref, hs_ref):
    n = pl.program_id(1)

    @pl.when(n == 0)
    def _():
        for s in range(CHUNK):
            z_ref[:, s * LANES:(s + 1) * LANES] = (
                u_ref[:, s].reshape(CHUNK_ROWS, LANES).astype(BF16))

    row = lax.broadcasted_iota(jnp.int32, (LANES, LANES), 0)
    lane = lax.broadcasted_iota(jnp.int32, (LANES, LANES), 1)
    own = (lane // SSM_GROUP) == (n * GROUPS_PER_TILE + row // SSM_STATE)
    for d in range(2):
        b_re = jnp.where(own, bbx_ref[0, d], 0.0)
        b_im = jnp.where(own, bbx_ref[1, d], 0.0)
        for s in range(CHUNK):
            e = CHUNK - 1 - s if d == 0 else s
            p_re = pwc_ref[0, d, :, e:e + 1]
            p_im = pwc_ref[1, d, :, e:e + 1]
            r0 = d * 2 * LANES
            bt_ref[r0:r0 + LANES, s * LANES:(s + 1) * LANES] = (
                p_re * b_re - p_im * b_im).astype(BF16)
            bt_ref[r0 + LANES:r0 + 2 * LANES, s * LANES:(s + 1) * LANES] = (
                p_re * b_im + p_im * b_re).astype(BF16)

    zs_ref[...] = lax.dot_general(z_ref[...], bt_ref[...], _NT, preferred_element_type=F32)

    fa_re, fa_im = a_ref[0, 0], a_ref[1, 0]
    ba_re, ba_im = a_ref[0, 1], a_ref[1, 1]

    def body(c, carry):
        f_re, f_im, b_re, b_im = carry
        rf = pl.multiple_of(c * BATCH, SUBLANES)
        rb = pl.multiple_of((N_CHUNKS - 1 - c) * BATCH, SUBLANES)
        hs_ref[pl.ds(rf, BATCH), 0:LANES] = f_re
        hs_ref[pl.ds(rf, BATCH), LANES:2 * LANES] = f_im
        hs_ref[pl.ds(rb, BATCH), 2 * LANES:3 * LANES] = b_re
        hs_ref[pl.ds(rb, BATCH), 3 * LANES:4 * LANES] = b_im
        zf_re = zs_ref[pl.ds(rf, BATCH), 0:LANES]
        zf_im = zs_ref[pl.ds(rf, BATCH), LANES:2 * LANES]
        zb_re = zs_ref[pl.ds(rb, BATCH), 2 * LANES:3 * LANES]
        zb_im = zs_ref[pl.ds(rb, BATCH), 3 * LANES:4 * LANES]
        return (fa_re * f_re - fa_im * f_im + zf_re, fa_re * f_im + fa_im * f_re + zf_im,
                ba_re * b_re - ba_im * b_im + zb_re, ba_re * b_im + ba_im * b_re + zb_im)

    zero = jnp.zeros((BATCH, LANES), F32)
    lax.fori_loop(0, N_CHUNKS, body, (zero, zero, zero, zero))
    h_ref[...] = hs_ref[...].astype(BF16)


def _ssm_state(u4, pwc, bbx, a16):
    tile_map = lambda g, n: (g, n, 0, 0, 0, 0)
    return pl.pallas_call(
        _ssm_state_kernel,
        grid=(N_LANE_BLOCKS, N_STATE_TILES),
        in_specs=[
            pl.BlockSpec((None, N_CHUNKS, CHUNK, BATCH, LANES), lambda g, n: (g, 0, 0, 0, 0)),
            pl.BlockSpec((None, None, 2, 2, LANES, CHUNK + 1), tile_map),
            pl.BlockSpec((None, None, 2, 2, LANES, LANES), tile_map),
            pl.BlockSpec((None, None, 2, 2, SUBLANES, LANES), tile_map),
        ],
        out_specs=[
            pl.BlockSpec((None, CHUNK_ROWS, ROW_W), lambda g, n: (g, 0, 0)),
            pl.BlockSpec((None, CHUNK_ROWS, STATE_TILE), lambda g, n: (g, 0, n)),
        ],
        out_shape=[
            jax.ShapeDtypeStruct((N_LANE_BLOCKS, CHUNK_ROWS, ROW_W), BF16),
            jax.ShapeDtypeStruct((N_LANE_BLOCKS, CHUNK_ROWS, STATE_W), BF16),
        ],
        scratch_shapes=[pltpu.VMEM((STATE_TILE, ROW_W), BF16),
                        pltpu.VMEM((CHUNK_ROWS, STATE_TILE), F32),
                        pltpu.VMEM((CHUNK_ROWS, STATE_TILE), F32)],
        compiler_params=pltpu.CompilerParams(
            dimension_semantics=("arbitrary", "arbitrary"),
            vmem_limit_bytes=VMEM_LIMIT),
        name="ssm_state",
    )(u4, pwc, bbx, a16)


def _ssm_out_kernel(z_ref, h_ref, tap_ref, rep_ref, cdup_ref, pwt_ref, ue_ref, d_ref, y_ref,
                    blk_ref, wt_ref, wc_ref):
    g = pl.program_id(0)
    n = pl.program_id(1)

    @pl.when(n == 0)
    def _():
        x = jnp.dot(tap_ref[...], rep_ref[...], preferred_element_type=F32)
        row = lax.broadcasted_iota(jnp.int32, x.shape, 0)
        lane = lax.broadcasted_iota(jnp.int32, x.shape, 1)
        own = ((row % LANES) // SSM_GROUP) == (lane // SSM_GROUP)
        blk_ref[...] = jnp.where(own, x, 0.0).astype(BF16)

    @pl.when((g == 0) & (n == 0))
    def _():
        wc_ref[...] = jnp.zeros_like(wc_ref)

    for tt in range(SLABS_PER_TILE):
        t = n * SLABS_PER_TILE + tt
        for s in range(CHUNK):
            off = pl.multiple_of((t - s + CHUNK - 1) * LANES, LANES)
            wt_ref[tt * LANES:(tt + 1) * LANES, s * LANES:(s + 1) * LANES] = (
                blk_ref[pl.ds(off, LANES), :])

    half = lax.broadcasted_iota(jnp.int32, (SSM_GROUP, LANES), 1) // SSM_STATE
    for tt in range(SLABS_PER_TILE):
        for h in range(GROUPS_PER_BLOCK):
            own = half == (h % GROUPS_PER_TILE)
            r0 = tt * LANES + h * SSM_GROUP
            for d in range(2):
                c_re = cdup_ref[0, d, h * SSM_GROUP:(h + 1) * SSM_GROUP, :]
                c_im = cdup_ref[1, d, h * SSM_GROUP:(h + 1) * SSM_GROUP, :]
                p_re = pwt_ref[tt, 0, d, h:h + 1, :]
                p_im = pwt_ref[tt, 1, d, h:h + 1, :]
                col = (h // GROUPS_PER_TILE) * STATE_TILE + d * 2 * LANES
                wc_ref[r0:r0 + SSM_GROUP, col:col + LANES] = jnp.where(
                    own, c_re * p_re - c_im * p_im, 0.0).astype(BF16)
                wc_ref[r0:r0 + SSM_GROUP, col + LANES:col + 2 * LANES] = jnp.where(
                    own, -(c_re * p_im + c_im * p_re), 0.0).astype(BF16)

    y = (lax.dot_general(z_ref[...], wt_ref[...], _NT, preferred_element_type=F32)
         + lax.dot_general(h_ref[...], wc_ref[...], _NT, preferred_element_type=F32))
    d_skip = d_ref[...]
    for tt in range(SLABS_PER_TILE):
        yt = y[:, tt * LANES:(tt + 1) * LANES].reshape(N_CHUNKS, BATCH, LANES)
        y_ref[:, tt] = jax.nn.gelu(yt + d_skip * ue_ref[:, tt])


def _ssm_out(zflat, hstate, tapr, rep, cdup, pwt, u4, d_b):
    assert SSM_GROUP == BF16_ROWS
    return pl.pallas_call(
        _ssm_out_kernel,
        grid=(N_LANE_BLOCKS, CHUNK // SLABS_PER_TILE),
        in_specs=[
            pl.BlockSpec((None, CHUNK_ROWS, ROW_W), lambda g, n: (g, 0, 0)),
            pl.BlockSpec((None, CHUNK_ROWS, STATE_W), lambda g, n: (g, 0, 0)),
            pl.BlockSpec((None, N_LAGS * LANES, SSM_GROUP), lambda g, n: (g, 0, 0)),
            pl.BlockSpec((SSM_GROUP, LANES), lambda g, n: (0, 0)),
            pl.BlockSpec((None, 2, 2, LANES, LANES), lambda g, n: (g, 0, 0, 0, 0)),
            pl.BlockSpec((None, SLABS_PER_TILE, 2, 2, GROUPS_PER_BLOCK, LANES),
                         lambda g, n: (g, n, 0, 0, 0, 0)),
            pl.BlockSpec((None, N_CHUNKS, SLABS_PER_TILE, BATCH, LANES),
                         lambda g, n: (g, 0, n, 0, 0)),
            pl.BlockSpec((None, SUBLANES, LANES), lambda g, n: (g, 0, 0)),
        ],
        out_specs=pl.BlockSpec((None, N_CHUNKS, SLABS_PER_TILE, BATCH, LANES),
                               lambda g, n: (g, 0, n, 0, 0)),
        out_shape=jax.ShapeDtypeStruct((N_LANE_BLOCKS, N_CHUNKS, CHUNK, BATCH, LANES), F32),
        scratch_shapes=[pltpu.VMEM((N_LAGS * LANES, LANES), BF16),
                        pltpu.VMEM((OUT_TILE, ROW_W), BF16),
                        pltpu.VMEM((OUT_TILE, STATE_W), BF16)],
        compiler_params=pltpu.CompilerParams(
            dimension_semantics=("arbitrary", "arbitrary"),
            vmem_limit_bytes=VMEM_LIMIT),
        name="ssm_out",
    )(zflat, hstate, tapr, rep, cdup, pwt, u4, d_b)


def _glu_kernel(y_ref, g_ref, w_ref, b_ref, o_ref, ys_ref, gs_ref):
    t_rows = TM_GLU // BATCH
    for j in range(N_LANE_BLOCKS):
        for b in range(BATCH):
            rows = slice(b * t_rows, (b + 1) * t_rows)
            cols = slice(j * LANES, (j + 1) * LANES)
            ys_ref[rows, cols] = y_ref[j, pl.ds(b, t_rows, stride=BATCH), :]
            gs_ref[rows, cols] = g_ref[j, pl.ds(b, t_rows, stride=BATCH), :]
    y = ys_ref[...]
    z = jnp.dot(y.astype(BF16), w_ref[...], preferred_element_type=F32) + b_ref[...]
    g = gs_ref[...]
    out = (y * jax.nn.sigmoid(z) * (g * jax.nn.sigmoid(g))).astype(BF16)
    o_ref[...] = out.reshape(BATCH, t_rows, D_SSM)


def _glu(y_slabs, g_slabs, w_bf16, b_row):
    slab_spec = pl.BlockSpec((N_LANE_BLOCKS, TM_GLU, LANES), lambda i: (0, i, 0))
    return pl.pallas_call(
        _glu_kernel,
        grid=(BATCH * SEQ // TM_GLU,),
        in_specs=[
            slab_spec,
            slab_spec,
            pl.BlockSpec((D_SSM, D_SSM), lambda i: (0, 0)),
            pl.BlockSpec((1, D_SSM), lambda i: (0, 0)),
        ],
        out_specs=pl.BlockSpec((BATCH, TM_GLU // BATCH, D_SSM), lambda i: (0, i, 0)),
        out_shape=jax.ShapeDtypeStruct((BATCH, SEQ, D_SSM), BF16),
        scratch_shapes=[pltpu.VMEM((TM_GLU, D_SSM), F32), pltpu.VMEM((TM_GLU, D_SSM), F32)],
        compiler_params=pltpu.CompilerParams(
            dimension_semantics=("parallel",), vmem_limit_bytes=VMEM_LIMIT),
        name="glu",
    )(y_slabs, g_slabs, w_bf16, b_row)


def _attn_kernel(lam_ref, q_ref, k_ref, v_ref, ga_ref, ng_ref, o_ref, *, post_scale):
    k = k_ref[...]
    v = v_ref[...]
    lam = lam_ref[0]
    lane = lax.broadcasted_iota(jnp.int32, (TQ, V_DIM), 1)
    zero = jnp.zeros((TQ, V_DIM), BF16)

    def scores(r):
        q = q_ref[r * TQ:(r + 1) * TQ, :]
        return (lax.dot_general(jnp.where(lane < HEAD_DIM, q, zero), k, _NT,
                                preferred_element_type=F32),
                lax.dot_general(jnp.where(lane >= HEAD_DIM, q, zero), k, _NT,
                                preferred_element_type=F32))

    def exp_sum(s):
        e = jnp.exp2(s - jnp.max(s, axis=1, keepdims=True))
        return e, jnp.sum(e, axis=1, keepdims=True)

    n_tiles = TQ_STEP // TQ
    s_next = scores(0)
    for r in range(n_tiles):
        s1, s2 = s_next
        if r + 1 < n_tiles:
            s_next = scores(r + 1)
        e1, l1 = exp_sum(s1)
        e2, l2 = exp_sum(s2)
        w = (e1 - (lam * l1 / l2) * e2).astype(BF16)
        o = jnp.dot(w, v, preferred_element_type=F32) / l1
        ms = jnp.mean(o * o, axis=1, keepdims=True)
        o = o * lax.rsqrt(ms + RMS_EPS) * ng_ref[...] * post_scale
        ga = ga_ref[r * TQ:(r + 1) * TQ, :]
        o_ref[r * TQ:(r + 1) * TQ, :] = (o * (ga * jax.nn.sigmoid(ga))).astype(BF16)


def _attention(lam, q, k, v, ga, ng_row, *, post_scale):
    nq = SEQ // TQ_STEP

    def q_map(b, h, i):
        return (b * nq + i, h)

    def kv_map(b, h, i):
        return (b, h)

    return pl.pallas_call(
        functools.partial(_attn_kernel, post_scale=post_scale),
        grid=(BATCH, N_HEADS, nq),
        in_specs=[
            pl.BlockSpec(memory_space=pltpu.SMEM),
            pl.BlockSpec((TQ_STEP, V_DIM), q_map),
            pl.BlockSpec((SEQ, V_DIM), kv_map),
            pl.BlockSpec((SEQ, V_DIM), kv_map),
            pl.BlockSpec((TQ_STEP, V_DIM), q_map),
            pl.BlockSpec((1, V_DIM), lambda b, h, i: (0, 0)),
        ],
        out_specs=pl.BlockSpec((TQ_STEP, V_DIM), q_map),
        out_shape=jax.ShapeDtypeStruct((BATCH * SEQ, D_ATTN), BF16),
        compiler_params=pltpu.CompilerParams(
            dimension_semantics=("parallel", "parallel", "arbitrary"),
            vmem_limit_bytes=VMEM_LIMIT),
        name="diff_attn",
    )(lam, q, k, v, ga, ng_row)


def _out_proj_kernel(ms_ref, ma_ref, w1_ref, w2_ref, x_ref, eg_ref, eb_ref, g_ref, b_ref,
                     o_ref, *maybe_ob_ref, pre_ln):
    for c in range(TM_OUT // OUT_ROW_CHUNK):
        rows = slice(c * OUT_ROW_CHUNK, (c + 1) * OUT_ROW_CHUNK)
        x = x_ref[rows, :]
        if pre_ln:
            x = _layer_norm(x, eg_ref[...], eb_ref[...])
        out = (jnp.dot(ms_ref[rows, :], w1_ref[...], preferred_element_type=F32)
               + jnp.dot(ma_ref[rows, :], w2_ref[...], preferred_element_type=F32))
        y = _layer_norm(DEEPNORM_ALPHA * x + out, g_ref[...], b_ref[...])
        o_ref[rows, :] = y
        for ob_ref in maybe_ob_ref:
            ob_ref[rows, :] = y.astype(BF16)


def _out_proj(mix_ssm, mix_attn, w_bf16, layer, x2d, eg, eb, g, b, *, pre_ln, emit_bf16):
    n_rows = BATCH * SEQ
    row = lambda i: (0, 0)
    nat = pl.BlockSpec((TM_OUT, D_MODEL), lambda i: (i, 0))
    out_specs = [nat]
    out_shape = [jax.ShapeDtypeStruct((n_rows, D_MODEL), F32)]
    if emit_bf16:
        out_specs.append(nat)
        out_shape.append(jax.ShapeDtypeStruct((n_rows, D_MODEL), BF16))
    return pl.pallas_call(
        functools.partial(_out_proj_kernel, pre_ln=pre_ln),
        grid=(n_rows // TM_OUT,),
        in_specs=[
            pl.BlockSpec((TM_OUT, D_SSM), lambda i: (i, 0)),
            pl.BlockSpec((TM_OUT, D_ATTN), lambda i: (i, 0)),
            pl.BlockSpec((None, D_SSM, D_MODEL), lambda i: (layer, 0, 0)),
            pl.BlockSpec((None, D_ATTN, D_MODEL), lambda i: (layer, 1, 0)),
            nat,
            pl.BlockSpec((1, D_MODEL), row),
            pl.BlockSpec((1, D_MODEL), row),
            pl.BlockSpec((1, D_MODEL), row),
            pl.BlockSpec((1, D_MODEL), row),
        ],
        out_specs=out_specs,
        out_shape=out_shape,
        compiler_params=pltpu.CompilerParams(
            dimension_semantics=("parallel",), vmem_limit_bytes=VMEM_LIMIT),
        name="out_proj",
    )(mix_ssm, mix_attn, w_bf16, w_bf16, x2d, eg, eb, g, b)


def _rotary_tables():
    pos = jnp.arange(SEQ, dtype=F32)
    inv_freq = ROPE_THETA ** (-jnp.arange(0, ROT_DIM, 2, dtype=F32) / ROT_DIM)
    ang = pos[:, None] * inv_freq[None, :]
    cos = jnp.cos(ang)
    sin = jnp.sin(ang)
    ones = jnp.ones((SEQ, HEAD_DIM - ROT_DIM), F32)
    zeros = jnp.zeros((SEQ, HEAD_DIM - ROT_DIM), F32)
    z8 = jnp.zeros((SEQ, ROT_HALF), F32)
    reps = LANES // HEAD_DIM
    cos_t = jnp.tile(jnp.concatenate([cos, cos, ones], axis=1), (1, reps))
    sin_a = jnp.tile(jnp.concatenate([-sin, z8, zeros], axis=1), (1, reps))
    sin_b = jnp.tile(jnp.concatenate([z8, sin, zeros], axis=1), (1, reps))
    return cos_t, sin_a, sin_b


def _ssm_tables(lam_re, lam_im, log_step, b_re, b_im, c_re, c_im):
    G, P, C, L = N_SSM_GROUPS, SSM_STATE, SSM_GROUP, CHUNK
    GB, NT, GT = N_LANE_BLOCKS, N_STATE_TILES, GROUPS_PER_TILE
    hi = lax.Precision.HIGHEST
    step = jnp.exp(log_step)[..., None]
    zr = lam_re * step
    zi = lam_im * step
    kpow = jnp.arange(L + 1, dtype=F32)[:, None, None, None]
    mag = jnp.exp(kpow * zr[None])
    pw = jnp.stack([mag * jnp.cos(kpow * zi[None]),
                    mag * jnp.sin(kpow * zi[None])])
    nr = pw[0, 1] - 1.0
    ni = pw[1, 1]
    den = lam_re * lam_re + lam_im * lam_im
    coef_re = (nr * lam_re + ni * lam_im) / den
    coef_im = (ni * lam_re - nr * lam_im) / den
    bb = jnp.stack([coef_re[..., None] * b_re - coef_im[..., None] * b_im,
                    coef_re[..., None] * b_im + coef_im[..., None] * b_re])

    m_re = pw[0, :L, ..., None] * bb[0][None] - pw[1, :L, ..., None] * bb[1][None]
    m_im = pw[0, :L, ..., None] * bb[1][None] + pw[1, :L, ..., None] * bb[0][None]
    taps = (jnp.einsum('dgop,kdgpi->kdgio', c_re, m_re, precision=hi)
            - jnp.einsum('dgop,kdgpi->kdgio', c_im, m_im, precision=hi))
    tap_lag = jnp.concatenate([taps[:0:-1, 1], (taps[0, 0] + taps[0, 1])[None], taps[1:, 0]])
    tapr = tap_lag.transpose(0, 1, 3, 2).reshape(N_LAGS, GB, LANES, C)
    tapr = tapr.transpose(1, 0, 2, 3).reshape(GB, N_LAGS * LANES, C).astype(BF16)

    pwc = pw.reshape(2, L + 1, 2, GB, NT, GT, P).transpose(3, 4, 0, 2, 5, 6, 1)
    pwc = pwc.reshape(GB, NT, 2, 2, LANES, L + 1)
    bbx = bb.reshape(2, 2, GB, NT, GT * P, C).transpose(2, 3, 0, 1, 4, 5)
    bbx = jnp.tile(bbx, (1, 1, 1, 1, 1, GROUPS_PER_BLOCK))
    a16 = pw[:, L].reshape(2, 2, GB, NT, 1, LANES).transpose(2, 3, 0, 1, 4, 5)
    a16 = jnp.broadcast_to(a16, (GB, NT, 2, 2, SUBLANES, LANES))

    cc = jnp.stack([c_re, c_im]).reshape(2, 2, GB, LANES, P).transpose(2, 0, 1, 3, 4)
    cdup = jnp.concatenate([cc, cc], axis=-1)
    ar_l = jnp.arange(L)
    pt = jnp.stack([pw[:, ar_l + 1, 0], pw[:, L - ar_l, 1]], axis=2)
    pt = pt.reshape(2, L, 2, GB, GROUPS_PER_BLOCK, P).transpose(3, 1, 0, 2, 4, 5)
    pwt = jnp.concatenate([pt, pt], axis=-1)
    return tapr, pwc, bbx, a16, cdup, pwt


def kernel(x, ln_emb_g, ln_emb_b, w_in, ssm_lam_re, ssm_lam_im, ssm_log_step, ssm_b_re,
           ssm_b_im, ssm_c_re, ssm_c_im, ssm_d, w_glu, b_glu, lambda_q1, lambda_k1,
           lambda_q2, lambda_k2, attn_norm_g, w_out, ln_g, ln_b):
    assert x.shape == (BATCH, SEQ, D_MODEL) and x.dtype == F32
    cos_t, sin_a, sin_b = _rotary_tables()
    eg = ln_emb_g.reshape(1, D_MODEL)
    eb = ln_emb_b.reshape(1, D_MODEL)
    x2d = x.reshape(BATCH * SEQ, D_MODEL)
    rep = (jnp.arange(LANES)[None, :] % SSM_GROUP == jnp.arange(SSM_GROUP)[:, None]).astype(BF16)

    w_in_bf16 = w_in.astype(BF16)
    w_out_bf16 = w_out.astype(BF16)
    xb = _emb_ln(x2d, eg, eb)
    for l in range(DEPTH):
        pre_ln = l == 0
        u_tb, gs_tb = _proj_pair(xb, w_in_bf16, l, 0, "ssm")
        q, k = _proj_pair(xb, w_in_bf16, l, 1, "qk", (cos_t, sin_a, sin_b))
        v, ga = _proj_pair(xb, w_in_bf16, l, 2, "vg")

        tapr, pwc, bbx, a16, cdup, pwt = _ssm_tables(
            ssm_lam_re[l], ssm_lam_im[l], ssm_log_step[l], ssm_b_re[l], ssm_b_im[l],
            ssm_c_re[l], ssm_c_im[l])
        u5 = u_tb.reshape(N_LANE_BLOCKS, N_CHUNKS, CHUNK, BATCH, LANES)
        zflat, hstate = _ssm_state(u5, pwc, bbx, a16)
        d_b = jnp.broadcast_to(ssm_d[l].reshape(N_LANE_BLOCKS, 1, LANES),
                               (N_LANE_BLOCKS, SUBLANES, LANES))
        y5 = _ssm_out(zflat, hstate, tapr, rep, cdup, pwt, u5, d_b)
        mix_ssm = _glu(y5.reshape(N_LANE_BLOCKS, SEQ * BATCH, LANES), gs_tb,
                       w_glu[l].astype(BF16), b_glu[l].reshape(1, D_SSM))

        lambda_init = 0.8 - 0.6 * math.exp(-0.3 * l)
        lam = (jnp.exp(jnp.sum(lambda_q1[l] * lambda_k1[l]))
               - jnp.exp(jnp.sum(lambda_q2[l] * lambda_k2[l])) + lambda_init)
        mix_attn = _attention(lam.reshape(1).astype(F32), q, k, v, ga,
                              attn_norm_g[l].reshape(1, V_DIM), post_scale=1.0 - lambda_init)

        outs = _out_proj(mix_ssm.reshape(BATCH * SEQ, D_SSM), mix_attn, w_out_bf16, l,
                         x2d, eg, eb, ln_g[l].reshape(1, D_MODEL), ln_b[l].reshape(1, D_MODEL),
                         pre_ln=pre_ln, emit_bf16=l + 1 < DEPTH)
        x2d = outs[0]
        if l + 1 < DEPTH:
            xb = outs[1]
    return x2d.reshape(BATCH, SEQ, D_MODEL)
```

```python
import functools
import math

import jax
import jax.numpy as jnp
from jax import lax
from jax.experimental import pallas as pl
from jax.experimental.pallas import tpu as pltpu

F32 = jnp.float32
BF16 = jnp.bfloat16

D_MODEL = 2048
BATCH = 8
SEQ = 2048
DEPTH = 2
D_SSM = 1024
D_ATTN = 1024
SSM_GROUP = 16
N_SSM_GROUPS = D_SSM // SSM_GROUP
SSM_STATE = 64
HEAD_DIM = 64
V_DIM = 2 * HEAD_DIM
N_HEADS = D_ATTN // V_DIM
ROT_DIM = HEAD_DIM // 4
ROT_HALF = ROT_DIM // 2
ROPE_THETA = 500000.0
LN_EPS = 1e-5
RMS_EPS = 1e-5
DEEPNORM_ALPHA = (2.0 * DEPTH) ** 0.25
PROJ_W = 1024

LANES = 128
SUBLANES = 8
BF16_ROWS = 16

CHUNK = 16
N_CHUNKS = SEQ // CHUNK
N_LAGS = 2 * CHUNK - 1
GROUPS_PER_BLOCK = LANES // SSM_GROUP
N_LANE_BLOCKS = D_SSM // LANES
ROW_W = CHUNK * LANES
GROUPS_PER_TILE = LANES // SSM_STATE
N_STATE_TILES = GROUPS_PER_BLOCK // GROUPS_PER_TILE
STATE_TILE = 2 * 2 * LANES
STATE_W = N_STATE_TILES * STATE_TILE
SLABS_PER_TILE = 4
OUT_TILE = SLABS_PER_TILE * LANES
CHUNK_ROWS = N_CHUNKS * BATCH

TM_PROJ = 512
PROJ_SLICE = 256
TM_GLU = 512
TM_OUT = 512
TQ = 256
TQ_STEP = 2048
Q_SCALE = HEAD_DIM ** -0.5 * math.log2(math.e)

VMEM_LIMIT = 56 * 1024 * 1024

_NT = (((1,), (1,)), ((), ()))


def _layer_norm(x, g, b):
    mu = jnp.mean(x, axis=-1, keepdims=True)
    xc = x - mu
    var = jnp.mean(xc * xc, axis=-1, keepdims=True)
    return xc * lax.rsqrt(var + LN_EPS) * g + b


def _emb_ln_kernel(x_ref, g_ref, b_ref, o_ref):
    o_ref[...] = _layer_norm(x_ref[...], g_ref[...], b_ref[...]).astype(BF16)


def _emb_ln(x2d, eg, eb):
    n_rows = BATCH * SEQ
    return pl.pallas_call(
        _emb_ln_kernel,
        grid=(n_rows // TM_PROJ,),
        in_specs=[
            pl.BlockSpec((TM_PROJ, D_MODEL), lambda i: (i, 0)),
            pl.BlockSpec((1, D_MODEL), lambda i: (0, 0)),
            pl.BlockSpec((1, D_MODEL), lambda i: (0, 0)),
        ],
        out_specs=pl.BlockSpec((TM_PROJ, D_MODEL), lambda i: (i, 0)),
        out_shape=jax.ShapeDtypeStruct((n_rows, D_MODEL), BF16),
        compiler_params=pltpu.CompilerParams(
            dimension_semantics=("parallel",), vmem_limit_bytes=VMEM_LIMIT),
        name="emb_ln",
    )(x2d, eg, eb)


def _proj_kernel(x_ref, w_ref, *refs, kind):
    if kind == "qk":
        cos_ref, sa_ref, sb_ref, oa_ref, ob_ref = refs
    else:
        oa_ref, ob_ref = refs
    x = x_ref[...].reshape(TM_PROJ, D_MODEL)
    slices_per_piece = PROJ_W // PROJ_SLICE
    for n in range(2 * slices_per_piece):
        acc = jnp.dot(x, w_ref[:, n * PROJ_SLICE:(n + 1) * PROJ_SLICE],
                      preferred_element_type=F32)
        first = n < slices_per_piece
        o_ref = oa_ref if first else ob_ref
        col = (n % slices_per_piece) * PROJ_SLICE
        if kind == "ssm":
            t_rows = TM_PROJ // BATCH
            for cb in range(PROJ_SLICE // LANES):
                for b in range(BATCH):
                    o_ref[col // LANES + cb, pl.ds(b, t_rows, stride=BATCH), :] = (
                        acc[b * t_rows:(b + 1) * t_rows, cb * LANES:(cb + 1) * LANES])
        elif kind == "qk":
            for cb in range(PROJ_SLICE // LANES):
                t = acc[:, cb * LANES:(cb + 1) * LANES]
                r = (t * cos_ref[...] + pltpu.roll(t, LANES - ROT_HALF, 1) * sa_ref[...]
                     + pltpu.roll(t, ROT_HALF, 1) * sb_ref[...])
                if first:
                    r = r * Q_SCALE
                o_ref[:, col + cb * LANES:col + (cb + 1) * LANES] = r.astype(BF16)
        else:
            o_ref[:, col:col + PROJ_SLICE] = acc.astype(o_ref.dtype)


def _proj_pair(xb, w_bf16, layer, pair, kind, tables=()):
    nt = SEQ // TM_PROJ
    n_rows = BATCH * SEQ
    nat_map = lambda i: (i, 0)
    if kind == "ssm":
        x_arg = xb.reshape(BATCH, SEQ, D_MODEL)
        x_spec = pl.BlockSpec((BATCH, TM_PROJ // BATCH, D_MODEL), lambda i: (0, i, 0))
        o_spec = pl.BlockSpec((N_LANE_BLOCKS, TM_PROJ, LANES), lambda i: (0, i, 0))
        shape, dtypes = (N_LANE_BLOCKS, n_rows, LANES), (F32, F32)
    else:
        x_arg = xb
        x_spec = pl.BlockSpec((TM_PROJ, D_MODEL), nat_map)
        o_spec = pl.BlockSpec((TM_PROJ, PROJ_W), nat_map)
        shape, dtypes = (n_rows, PROJ_W), ((BF16, BF16) if kind == "qk" else (BF16, F32))
    table_specs = [pl.BlockSpec((TM_PROJ, LANES), lambda i: (i % nt, 0)) for _ in tables]
    return pl.pallas_call(
        functools.partial(_proj_kernel, kind=kind),
        grid=(n_rows // TM_PROJ,),
        in_specs=[
            x_spec,
            pl.BlockSpec((None, D_MODEL, 2 * PROJ_W), lambda i: (layer, 0, pair)),
        ] + table_specs,
        out_specs=[o_spec] * 2,
        out_shape=[jax.ShapeDtypeStruct(shape, dt) for dt in dtypes],
        compiler_params=pltpu.CompilerParams(
            dimension_semantics=("parallel",), vmem_limit_bytes=VMEM_LIMIT),
        name="proj_" + kind,
    )(x_arg, w_bf16, *tables)


def _ssm_state_kernel(u_ref, pwr_ref, bbt_ref, z_ref, h_ref, bk_ref, zs_ref, hs_ref):
    n = pl.program_id(1)

    @pl.when(n == 0)
    def _():
        for s in range(CHUNK):
            z_ref[:, s * LANES:(s + 1) * LANES] = (
                u_ref[:, s].reshape(CHUNK_ROWS, LANES).astype(BF16))

    row = lax.broadcasted_iota(jnp.int32, (LANES, LANES), 0)
    lane = lax.broadcasted_iota(jnp.int32, (LANES, LANES), 1)
    own = (row // SSM_GROUP) == (n * GROUPS_PER_TILE + lane // SSM_STATE)
    for d in range(2):
        b_re = jnp.where(own, bbt_ref[0, d], 0.0)
        b_im = jnp.where(own, bbt_ref[1, d], 0.0)
        for s in range(CHUNK):
            e = CHUNK - 1 - s if d == 0 else s
            p_re = pwr_ref[0, d, e:e + 1, :]
            p_im = pwr_ref[1, d, e:e + 1, :]
            c0 = d * 2 * LANES
            bk_ref[s * LANES:(s + 1) * LANES, c0:c0 + LANES] = (
                p_re * b_re - p_im * b_im).astype(BF16)
            bk_ref[s * LANES:(s + 1) * LANES, c0 + LANES:c0 + 2 * LANES] = (
                p_re * b_im + p_im * b_re).astype(BF16)

    zs_ref[...] = jnp.dot(z_ref[...], bk_ref[...], preferred_element_type=F32)

    decay = lambda ri, d: jnp.broadcast_to(pwr_ref[ri, d, CHUNK:CHUNK + 1, :], (BATCH, LANES))
    fa_re, fa_im = decay(0, 0), decay(1, 0)
    ba_re, ba_im = decay(0, 1), decay(1, 1)

    def body(c, carry):
        f_re, f_im, b_re, b_im = carry
        rf = pl.multiple_of(c * BATCH, SUBLANES)
        rb = pl.multiple_of((N_CHUNKS - 1 - c) * BATCH, SUBLANES)
        hs_ref[pl.ds(rf, BATCH), 0:LANES] = f_re
        hs_ref[pl.ds(rf, BATCH), LANES:2 * LANES] = f_im
        hs_ref[pl.ds(rb, BATCH), 2 * LANES:3 * LANES] = b_re
        hs_ref[pl.ds(rb, BATCH), 3 * LANES:4 * LANES] = b_im
        zf_re = zs_ref[pl.ds(rf, BATCH), 0:LANES]
        zf_im = zs_ref[pl.ds(rf, BATCH), LANES:2 * LANES]
        zb_re = zs_ref[pl.ds(rb, BATCH), 2 * LANES:3 * LANES]
        zb_im = zs_ref[pl.ds(rb, BATCH), 3 * LANES:4 * LANES]
        return (fa_re * f_re - fa_im * f_im + zf_re, fa_re * f_im + fa_im * f_re + zf_im,
                ba_re * b_re - ba_im * b_im + zb_re, ba_re * b_im + ba_im * b_re + zb_im)

    zero = jnp.zeros((BATCH, LANES), F32)
    lax.fori_loop(0, N_CHUNKS, body, (zero, zero, zero, zero))
    h_ref[...] = hs_ref[...].astype(BF16)


def _ssm_state(u5, pwr, bbt):
    tile_map = lambda g, n: (g, n, 0, 0, 0, 0)
    return pl.pallas_call(
        _ssm_state_kernel,
        grid=(N_LANE_BLOCKS, N_STATE_TILES),
        in_specs=[
            pl.BlockSpec((None, N_CHUNKS, CHUNK, BATCH, LANES), lambda g, n: (g, 0, 0, 0, 0)),
            pl.BlockSpec((None, None, 2, 2, CHUNK + 1, LANES), tile_map),
            pl.BlockSpec((None, None, 2, 2, LANES, LANES), tile_map),
        ],
        out_specs=[
            pl.BlockSpec((None, CHUNK_ROWS, ROW_W), lambda g, n: (g, 0, 0)),
            pl.BlockSpec((None, CHUNK_ROWS, STATE_TILE), lambda g, n: (g, 0, n)),
        ],
        out_shape=[
            jax.ShapeDtypeStruct((N_LANE_BLOCKS, CHUNK_ROWS, ROW_W), BF16),
            jax.ShapeDtypeStruct((N_LANE_BLOCKS, CHUNK_ROWS, STATE_W), BF16),
        ],
        scratch_shapes=[pltpu.VMEM((ROW_W, STATE_TILE), BF16),
                        pltpu.VMEM((CHUNK_ROWS, STATE_TILE), F32),
                        pltpu.VMEM((CHUNK_ROWS, STATE_TILE), F32)],
        compiler_params=pltpu.CompilerParams(
            dimension_semantics=("arbitrary", "arbitrary"),
            vmem_limit_bytes=VMEM_LIMIT),
        name="ssm_state",
    )(u5, pwr, bbt)


def _ssm_out_kernel(z_ref, h_ref, tap_ref, rep_ref, cdup_ref, pwt_ref, ue_ref, d_ref, y_ref,
                    blk_ref, wt_ref, wc_ref):
    g = pl.program_id(0)
    n = pl.program_id(1)

    @pl.when(n == 0)
    def _():
        x = jnp.dot(tap_ref[...], rep_ref[...], preferred_element_type=F32)
        row = lax.broadcasted_iota(jnp.int32, x.shape, 0)
        lane = lax.broadcasted_iota(jnp.int32, x.shape, 1)
        own = ((row % LANES) // SSM_GROUP) == (lane // SSM_GROUP)
        blk_ref[...] = jnp.where(own, x, 0.0).astype(BF16)

    @pl.when((g == 0) & (n == 0))
    def _():
        wc_ref[...] = jnp.zeros_like(wc_ref)

    for tt in range(SLABS_PER_TILE):
        t = n * SLABS_PER_TILE + tt
        for s in range(CHUNK):
            off = pl.multiple_of((t - s + CHUNK - 1) * LANES, LANES)
            wt_ref[tt * LANES:(tt + 1) * LANES, s * LANES:(s + 1) * LANES] = (
                blk_ref[pl.ds(off, LANES), :])

    half = lax.broadcasted_iota(jnp.int32, (SSM_GROUP, LANES), 1) // SSM_STATE
    for tt in range(SLABS_PER_TILE):
        for h in range(GROUPS_PER_BLOCK):
            own = half == (h % GROUPS_PER_TILE)
            r0 = tt * LANES + h * SSM_GROUP
            for d in range(2):
                c_re = cdup_ref[0, d, h * SSM_GROUP:(h + 1) * SSM_GROUP, :]
                c_im = cdup_ref[1, d, h * SSM_GROUP:(h + 1) * SSM_GROUP, :]
                p_re = pwt_ref[tt, 0, d, h:h + 1, :]
                p_im = pwt_ref[tt, 1, d, h:h + 1, :]
                col = (h // GROUPS_PER_TILE) * STATE_TILE + d * 2 * LANES
                wc_ref[r0:r0 + SSM_GROUP, col:col + LANES] = jnp.where(
                    own, c_re * p_re - c_im * p_im, 0.0).astype(BF16)
                wc_ref[r0:r0 + SSM_GROUP, col + LANES:col + 2 * LANES] = jnp.where(
                    own, -(c_re * p_im + c_im * p_re), 0.0).astype(BF16)

    y = (lax.dot_general(z_ref[...], wt_ref[...], _NT, preferred_element_type=F32)
         + lax.dot_general(h_ref[...], wc_ref[...], _NT, preferred_element_type=F32))
    d_skip = d_ref[...]
    for tt in range(SLABS_PER_TILE):
        yt = y[:, tt * LANES:(tt + 1) * LANES].reshape(N_CHUNKS, BATCH, LANES)
        y_ref[:, tt] = jax.nn.gelu(yt + d_skip * ue_ref[:, tt])


def _ssm_out(zflat, hstate, tapr, rep, cdup, pwt, u4, d_b):
    assert SSM_GROUP == BF16_ROWS
    return pl.pallas_call(
        _ssm_out_kernel,
        grid=(N_LANE_BLOCKS, CHUNK // SLABS_PER_TILE),
        in_specs=[
            pl.BlockSpec((None, CHUNK_ROWS, ROW_W), lambda g, n: (g, 0, 0)),
            pl.BlockSpec((None, CHUNK_ROWS, STATE_W), lambda g, n: (g, 0, 0)),
            pl.BlockSpec((None, N_LAGS * LANES, SSM_GROUP), lambda g, n: (g, 0, 0)),
            pl.BlockSpec((SSM_GROUP, LANES), lambda g, n: (0, 0)),
            pl.BlockSpec((None, 2, 2, LANES, LANES), lambda g, n: (g, 0, 0, 0, 0)),
            pl.BlockSpec((None, SLABS_PER_TILE, 2, 2, GROUPS_PER_BLOCK, LANES),
                         lambda g, n: (g, n, 0, 0, 0, 0)),
            pl.BlockSpec((None, N_CHUNKS, SLABS_PER_TILE, BATCH, LANES),
                         lambda g, n: (g, 0, n, 0, 0)),
            pl.BlockSpec((None, SUBLANES, LANES), lambda g, n: (g, 0, 0)),
        ],
        out_specs=pl.BlockSpec((None, N_CHUNKS, SLABS_PER_TILE, BATCH, LANES),
                               lambda g, n: (g, 0, n, 0, 0)),
        out_shape=jax.ShapeDtypeStruct((N_LANE_BLOCKS, N_CHUNKS, CHUNK, BATCH, LANES), F32),
        scratch_shapes=[pltpu.VMEM((N_LAGS * LANES, LANES), BF16),
                        pltpu.VMEM((OUT_TILE, ROW_W), BF16),
                        pltpu.VMEM((OUT_TILE, STATE_W), BF16)],
        compiler_params=pltpu.CompilerParams(
            dimension_semantics=("arbitrary", "arbitrary"),
            vmem_limit_bytes=VMEM_LIMIT),
        name="ssm_out",
    )(zflat, hstate, tapr, rep, cdup, pwt, u4, d_b)


def _glu_kernel(y_ref, g_ref, w_ref, b_ref, o_ref, ys_ref, gs_ref):
    t_rows = TM_GLU // BATCH
    for j in range(N_LANE_BLOCKS):
        for b in range(BATCH):
            rows = slice(b * t_rows, (b + 1) * t_rows)
            cols = slice(j * LANES, (j + 1) * LANES)
            ys_ref[rows, cols] = y_ref[j, pl.ds(b, t_rows, stride=BATCH), :]
            gs_ref[rows, cols] = g_ref[j, pl.ds(b, t_rows, stride=BATCH), :]
    y = ys_ref[...]
    z = jnp.dot(y.astype(BF16), w_ref[...], preferred_element_type=F32) + b_ref[...]
    g = gs_ref[...]
    out = (y * jax.nn.sigmoid(z) * (g * jax.nn.sigmoid(g))).astype(BF16)
    o_ref[...] = out.reshape(BATCH, t_rows, D_SSM)


def _glu(y_slabs, g_slabs, w_bf16, b_row):
    slab_spec = pl.BlockSpec((N_LANE_BLOCKS, TM_GLU, LANES), lambda i: (0, i, 0))
    return pl.pallas_call(
        _glu_kernel,
        grid=(BATCH * SEQ // TM_GLU,),
        in_specs=[
            slab_spec,
            slab_spec,
            pl.BlockSpec((D_SSM, D_SSM), lambda i: (0, 0)),
            pl.BlockSpec((1, D_SSM), lambda i: (0, 0)),
        ],
        out_specs=pl.BlockSpec((BATCH, TM_GLU // BATCH, D_SSM), lambda i: (0, i, 0)),
        out_shape=jax.ShapeDtypeStruct((BATCH, SEQ, D_SSM), BF16),
        scratch_shapes=[pltpu.VMEM((TM_GLU, D_SSM), F32), pltpu.VMEM((TM_GLU, D_SSM), F32)],
        compiler_params=pltpu.CompilerParams(
            dimension_semantics=("parallel",), vmem_limit_bytes=VMEM_LIMIT),
        name="glu",
    )(y_slabs, g_slabs, w_bf16, b_row)


def _attn_kernel(lam_ref, q_ref, k_ref, v_ref, ga_ref, ng_ref, o_ref, *, post_scale):
    k = k_ref[...]
    v = v_ref[...]
    lam = lam_ref[0]
    lane = lax.broadcasted_iota(jnp.int32, (TQ, V_DIM), 1)
    zero = jnp.zeros((TQ, V_DIM), BF16)

    def scores(r):
        q = q_ref[r * TQ:(r + 1) * TQ, :]
        return (lax.dot_general(jnp.where(lane < HEAD_DIM, q, zero), k, _NT,
                                preferred_element_type=F32),
                lax.dot_general(jnp.where(lane >= HEAD_DIM, q, zero), k, _NT,
                                preferred_element_type=F32))

    def exp_sum(s):
        e = jnp.exp2(s - jnp.max(s, axis=1, keepdims=True))
        return e, jnp.sum(e, axis=1, keepdims=True)

    n_tiles = TQ_STEP // TQ
    s_next = scores(0)
    for r in range(n_tiles):
        s1, s2 = s_next
        if r + 1 < n_tiles:
            s_next = scores(r + 1)
        e1, l1 = exp_sum(s1)
        e2, l2 = exp_sum(s2)
        w = (e1 - (lam * l1 / l2) * e2).astype(BF16)
        o = jnp.dot(w, v, preferred_element_type=F32) / l1
        ms = jnp.mean(o * o, axis=1, keepdims=True)
        o = o * lax.rsqrt(ms + RMS_EPS) * ng_ref[...] * post_scale
        ga = ga_ref[r * TQ:(r + 1) * TQ, :]
        o_ref[r * TQ:(r + 1) * TQ, :] = (o * (ga * jax.nn.sigmoid(ga))).astype(BF16)


def _attention(lam, q, k, v, ga, ng_row, *, post_scale):
    nq = SEQ // TQ_STEP

    def q_map(b, h, i):
        return (b * nq + i, h)

    def kv_map(b, h, i):
        return (b, h)

    return pl.pallas_call(
        functools.partial(_attn_kernel, post_scale=post_scale),
        grid=(BATCH, N_HEADS, nq),
        in_specs=[
            pl.BlockSpec(memory_space=pltpu.SMEM),
            pl.BlockSpec((TQ_STEP, V_DIM), q_map),
            pl.BlockSpec((SEQ, V_DIM), kv_map),
            pl.BlockSpec((SEQ, V_DIM), kv_map),
            pl.BlockSpec((TQ_STEP, V_DIM), q_map),
            pl.BlockSpec((1, V_DIM), lambda b, h, i: (0, 0)),
        ],
        out_specs=pl.BlockSpec((TQ_STEP, V_DIM), q_map),
        out_shape=jax.ShapeDtypeStruct((BATCH * SEQ, D_ATTN), BF16),
        compiler_params=pltpu.CompilerParams(
            dimension_semantics=("parallel", "parallel", "arbitrary"),
            vmem_limit_bytes=VMEM_LIMIT),
        name="diff_attn",
    )(lam, q, k, v, ga, ng_row)


def _out_proj_kernel(ms_ref, ma_ref, w1_ref, w2_ref, x_ref, eg_ref, eb_ref, g_ref, b_ref,
                     o_ref, *maybe_ob_ref, pre_ln):
    for n in range(D_MODEL // PROJ_SLICE):
        cols = slice(n * PROJ_SLICE, (n + 1) * PROJ_SLICE)
        o_ref[:, cols] = (
            jnp.dot(ms_ref[...], w1_ref[:, cols], preferred_element_type=F32)
            + jnp.dot(ma_ref[...], w2_ref[:, cols], preferred_element_type=F32))
    x = x_ref[...]
    if pre_ln:
        x = _layer_norm(x, eg_ref[...], eb_ref[...])
    y = _layer_norm(DEEPNORM_ALPHA * x + o_ref[...], g_ref[...], b_ref[...])
    o_ref[...] = y
    for ob_ref in maybe_ob_ref:
        ob_ref[...] = y.astype(BF16)


def _out_proj(mix_ssm, mix_attn, w_bf16, layer, x2d, eg, eb, g, b, *, pre_ln, emit_bf16):
    n_rows = BATCH * SEQ
    row = lambda i: (0, 0)
    nat = lambda i: (i, 0)
    out_specs = [pl.BlockSpec((TM_OUT, D_MODEL), nat)]
    out_shape = [jax.ShapeDtypeStruct((n_rows, D_MODEL), F32)]
    if emit_bf16:
        out_specs.append(pl.BlockSpec((TM_OUT, D_MODEL), nat))
        out_shape.append(jax.ShapeDtypeStruct((n_rows, D_MODEL), BF16))
    return pl.pallas_call(
        functools.partial(_out_proj_kernel, pre_ln=pre_ln),
        grid=(n_rows // TM_OUT,),
        in_specs=[
            pl.BlockSpec((TM_OUT, D_SSM), nat),
            pl.BlockSpec((TM_OUT, D_ATTN), nat),
            pl.BlockSpec((None, D_SSM, D_MODEL), lambda i: (layer, 0, 0)),
            pl.BlockSpec((None, D_ATTN, D_MODEL), lambda i: (layer, 1, 0)),
            pl.BlockSpec((TM_OUT, D_MODEL), nat),
            pl.BlockSpec((1, D_MODEL), row),
            pl.BlockSpec((1, D_MODEL), row),
            pl.BlockSpec((1, D_MODEL), row),
            pl.BlockSpec((1, D_MODEL), row),
        ],
        out_specs=out_specs,
        out_shape=out_shape,
        compiler_params=pltpu.CompilerParams(
            dimension_semantics=("parallel",), vmem_limit_bytes=VMEM_LIMIT),
        name="out_proj",
    )(mix_ssm, mix_attn, w_bf16, w_bf16, x2d, eg, eb, g, b)


def _rotary_tables():
    pos = jnp.arange(SEQ, dtype=F32)
    inv_freq = ROPE_THETA ** (-jnp.arange(0, ROT_DIM, 2, dtype=F32) / ROT_DIM)
    ang = pos[:, None] * inv_freq[None, :]
    cos = jnp.cos(ang)
    sin = jnp.sin(ang)
    ones = jnp.ones((SEQ, HEAD_DIM - ROT_DIM), F32)
    zeros = jnp.zeros((SEQ, HEAD_DIM - ROT_DIM), F32)
    z8 = jnp.zeros((SEQ, ROT_HALF), F32)
    reps = LANES // HEAD_DIM
    cos_t = jnp.tile(jnp.concatenate([cos, cos, ones], axis=1), (1, reps))
    sin_a = jnp.tile(jnp.concatenate([-sin, z8, zeros], axis=1), (1, reps))
    sin_b = jnp.tile(jnp.concatenate([z8, sin, zeros], axis=1), (1, reps))
    return cos_t, sin_a, sin_b


def _ssm_tables(lam_re, lam_im, log_step, b_re, b_im, c_re, c_im):
    G, P, C, L = N_SSM_GROUPS, SSM_STATE, SSM_GROUP, CHUNK
    GB, NT, GT = N_LANE_BLOCKS, N_STATE_TILES, GROUPS_PER_TILE
    hi = lax.Precision.HIGHEST
    step = jnp.exp(log_step)[..., None]
    zr = lam_re * step
    zi = lam_im * step
    kpow = jnp.arange(L + 1, dtype=F32)[:, None, None, None]
    mag = jnp.exp(kpow * zr[None])
    pw = jnp.stack([mag * jnp.cos(kpow * zi[None]),
                    mag * jnp.sin(kpow * zi[None])])
    nr = pw[0, 1] - 1.0
    ni = pw[1, 1]
    den = lam_re * lam_re + lam_im * lam_im
    coef_re = (nr * lam_re + ni * lam_im) / den
    coef_im = (ni * lam_re - nr * lam_im) / den
    bb = jnp.stack([coef_re[..., None] * b_re - coef_im[..., None] * b_im,
                    coef_re[..., None] * b_im + coef_im[..., None] * b_re])

    m_re = pw[0, :L, ..., None] * bb[0][None] - pw[1, :L, ..., None] * bb[1][None]
    m_im = pw[0, :L, ..., None] * bb[1][None] + pw[1, :L, ..., None] * bb[0][None]
    taps = (jnp.einsum('dgop,kdgpi->kdgio', c_re, m_re, precision=hi)
            - jnp.einsum('dgop,kdgpi->kdgio', c_im, m_im, precision=hi))
    tap_lag = jnp.concatenate([taps[:0:-1, 1], (taps[0, 0] + taps[0, 1])[None], taps[1:, 0]])
    tapr = tap_lag.transpose(0, 1, 3, 2).reshape(N_LAGS, GB, LANES, C)
    tapr = tapr.transpose(1, 0, 2, 3).reshape(GB, N_LAGS * LANES, C).astype(BF16)

    pwr = pw.reshape(2, L + 1, 2, GB, NT, GT * P).transpose(3, 4, 0, 2, 1, 5)
    bbt = bb.reshape(2, 2, GB, NT, GT, P, C).transpose(2, 3, 0, 1, 6, 4, 5)
    bbt = jnp.tile(bbt.reshape(GB, NT, 2, 2, C, LANES), (1, 1, 1, 1, GROUPS_PER_BLOCK, 1))

    cc = jnp.stack([c_re, c_im]).reshape(2, 2, GB, LANES, P).transpose(2, 0, 1, 3, 4)
    cdup = jnp.concatenate([cc, cc], axis=-1)
    ar_l = jnp.arange(L)
    pt = jnp.stack([pw[:, ar_l + 1, 0], pw[:, L - ar_l, 1]], axis=2)
    pt = pt.reshape(2, L, 2, GB, GROUPS_PER_BLOCK, P).transpose(3, 1, 0, 2, 4, 5)
    pwt = jnp.concatenate([pt, pt], axis=-1)
    return tapr, pwr, bbt, cdup, pwt


def kernel(x, ln_emb_g, ln_emb_b, w_in, ssm_lam_re, ssm_lam_im, ssm_log_step, ssm_b_re,
           ssm_b_im, ssm_c_re, ssm_c_im, ssm_d, w_glu, b_glu, lambda_q1, lambda_k1,
           lambda_q2, lambda_k2, attn_norm_g, w_out, ln_g, ln_b):
    assert x.shape == (BATCH, SEQ, D_MODEL) and x.dtype == F32
    cos_t, sin_a, sin_b = _rotary_tables()
    eg = ln_emb_g.reshape(1, D_MODEL)
    eb = ln_emb_b.reshape(1, D_MODEL)
    x2d = x.reshape(BATCH * SEQ, D_MODEL)
    rep = (jnp.arange(LANES)[None, :] % SSM_GROUP == jnp.arange(SSM_GROUP)[:, None]).astype(BF16)

    w_in_bf16 = w_in.astype(BF16)
    w_out_bf16 = w_out.astype(BF16)
    xb = _emb_ln(x2d, eg, eb)
    for l in range(DEPTH):
        pre_ln = l == 0
        u_tb, gs_tb = _proj_pair(xb, w_in_bf16, l, 0, "ssm")
        q, k = _proj_pair(xb, w_in_bf16, l, 1, "qk", (cos_t, sin_a, sin_b))
        v, ga = _proj_pair(xb, w_in_bf16, l, 2, "vg")

        tapr, pwr, bbt, cdup, pwt = _ssm_tables(
            ssm_lam_re[l], ssm_lam_im[l], ssm_log_step[l], ssm_b_re[l], ssm_b_im[l],
            ssm_c_re[l], ssm_c_im[l])
        u5 = u_tb.reshape(N_LANE_BLOCKS, N_CHUNKS, CHUNK, BATCH, LANES)
        zflat, hstate = _ssm_state(u5, pwr, bbt)
        d_b = jnp.broadcast_to(ssm_d[l].reshape(N_LANE_BLOCKS, 1, LANES),
                               (N_LANE_BLOCKS, SUBLANES, LANES))
        y5 = _ssm_out(zflat, hstate, tapr, rep, cdup, pwt, u5, d_b)
        mix_ssm = _glu(y5.reshape(N_LANE_BLOCKS, SEQ * BATCH, LANES), gs_tb,
                       w_glu[l].astype(BF16), b_glu[l].reshape(1, D_SSM))

        lambda_init = 0.8 - 0.6 * math.exp(-0.3 * l)
        lam = (jnp.exp(jnp.sum(lambda_q1[l] * lambda_k1[l]))
               - jnp.exp(jnp.sum(lambda_q2[l] * lambda_k2[l])) + lambda_init)
        mix_attn = _attention(lam.reshape(1).astype(F32), q, k, v, ga,
                              attn_norm_g[l].reshape(1, V_DIM), post_scale=1.0 - lambda_init)

        outs = _out_proj(mix_ssm.reshape(BATCH * SEQ, D_SSM), mix_attn, w_out_bf16, l,
                         x2d, eg, eb, ln_g[l].reshape(1, D_MODEL), ln_b[l].reshape(1, D_MODEL),
                         pre_ln=pre_ln, emit_bf16=l + 1 < DEPTH)
        x2d = outs[0]
        if l + 1 < DEPTH:
            xb = outs[1]
    return x2d.reshape(BATCH, SEQ, D_MODEL)
```

```python
import functools
import math

import jax
import jax.numpy as jnp
from jax import lax
from jax.experimental import pallas as pl
from jax.experimental.pallas import tpu as pltpu

F32 = jnp.float32
BF16 = jnp.bfloat16

D_MODEL = 2048
BATCH = 8
SEQ = 2048
DEPTH = 2
D_SSM = 1024
D_ATTN = 1024
SSM_GROUP = 16
N_SSM_GROUPS = D_SSM // SSM_GROUP
SSM_STATE = 64
HEAD_DIM = 64
V_DIM = 2 * HEAD_DIM
N_HEADS = D_ATTN // V_DIM
ROT_DIM = HEAD_DIM // 4
ROT_HALF = ROT_DIM // 2
ROPE_THETA = 500000.0
LN_EPS = 1e-5
RMS_EPS = 1e-5
DEEPNORM_ALPHA = (2.0 * DEPTH) ** 0.25
PROJ_W = 1024

LANES = 128
SUBLANES = 8
BF16_ROWS = 16

CHUNK = 16
N_CHUNKS = SEQ // CHUNK
N_LAGS = 2 * CHUNK - 1
GROUPS_PER_BLOCK = LANES // SSM_GROUP
N_LANE_BLOCKS = D_SSM // LANES
ROW_W = CHUNK * LANES
GROUPS_PER_TILE = LANES // SSM_STATE
N_STATE_TILES = GROUPS_PER_BLOCK // GROUPS_PER_TILE
STATE_TILE = 2 * 2 * LANES
STATE_W = N_STATE_TILES * STATE_TILE
SLABS_PER_TILE = 4
OUT_TILE = SLABS_PER_TILE * LANES
CHUNK_ROWS = N_CHUNKS * BATCH

TM_PROJ = 512
PROJ_SLICE = 256
TM_GLU = 512
TM_OUT = 512
TQ = 512
HEADS_PER_STEP = 1
Q_SCALE = HEAD_DIM ** -0.5 * math.log2(math.e)

VMEM_LIMIT = 56 * 1024 * 1024

_NT = (((1,), (1,)), ((), ()))


def _layer_norm(x, g, b):
    mu = jnp.mean(x, axis=-1, keepdims=True)
    xc = x - mu
    var = jnp.mean(xc * xc, axis=-1, keepdims=True)
    return xc * lax.rsqrt(var + LN_EPS) * g + b


def _emb_ln_kernel(x_ref, g_ref, b_ref, o_ref):
    o_ref[...] = _layer_norm(x_ref[...], g_ref[...], b_ref[...]).astype(BF16)


def _emb_ln(x2d, eg, eb):
    n_rows = BATCH * SEQ
    return pl.pallas_call(
        _emb_ln_kernel,
        grid=(n_rows // TM_PROJ,),
        in_specs=[
            pl.BlockSpec((TM_PROJ, D_MODEL), lambda i: (i, 0)),
            pl.BlockSpec((1, D_MODEL), lambda i: (0, 0)),
            pl.BlockSpec((1, D_MODEL), lambda i: (0, 0)),
        ],
        out_specs=pl.BlockSpec((TM_PROJ, D_MODEL), lambda i: (i, 0)),
        out_shape=jax.ShapeDtypeStruct((n_rows, D_MODEL), BF16),
        compiler_params=pltpu.CompilerParams(
            dimension_semantics=("parallel",), vmem_limit_bytes=VMEM_LIMIT),
        name="emb_ln",
    )(x2d, eg, eb)


def _proj_kernel(x_ref, w_ref, *refs, kind):
    if kind == "qk":
        cos_ref, sa_ref, sb_ref, oa_ref, ob_ref = refs
    else:
        oa_ref, ob_ref = refs
    x = x_ref[...].reshape(TM_PROJ, D_MODEL)
    slices_per_piece = PROJ_W // PROJ_SLICE
    for n in range(2 * slices_per_piece):
        acc = jnp.dot(x, w_ref[:, n * PROJ_SLICE:(n + 1) * PROJ_SLICE],
                      preferred_element_type=F32)
        first = n < slices_per_piece
        o_ref = oa_ref if first else ob_ref
        col = (n % slices_per_piece) * PROJ_SLICE
        if kind == "ssm":
            t_rows = TM_PROJ // BATCH
            for cb in range(PROJ_SLICE // LANES):
                for b in range(BATCH):
                    o_ref[col // LANES + cb, pl.ds(b, t_rows, stride=BATCH), :] = (
                        acc[b * t_rows:(b + 1) * t_rows, cb * LANES:(cb + 1) * LANES])
        elif kind == "qk":
            for cb in range(PROJ_SLICE // LANES):
                t = acc[:, cb * LANES:(cb + 1) * LANES]
                r = (t * cos_ref[...] + pltpu.roll(t, LANES - ROT_HALF, 1) * sa_ref[...]
                     + pltpu.roll(t, ROT_HALF, 1) * sb_ref[...])
                if first:
                    r = r * Q_SCALE
                o_ref[:, col + cb * LANES:col + (cb + 1) * LANES] = r.astype(BF16)
        else:
            o_ref[:, col:col + PROJ_SLICE] = acc.astype(o_ref.dtype)


def _proj_pair(xb, w_bf16, layer, pair, kind, tables=()):
    nt = SEQ // TM_PROJ
    n_rows = BATCH * SEQ
    nat_map = lambda i: (i, 0)
    if kind == "ssm":
        x_arg = xb.reshape(BATCH, SEQ, D_MODEL)
        x_spec = pl.BlockSpec((BATCH, TM_PROJ // BATCH, D_MODEL), lambda i: (0, i, 0))
        o_spec = pl.BlockSpec((N_LANE_BLOCKS, TM_PROJ, LANES), lambda i: (0, i, 0))
        shape, dtypes = (N_LANE_BLOCKS, n_rows, LANES), (F32, F32)
    else:
        x_arg = xb
        x_spec = pl.BlockSpec((TM_PROJ, D_MODEL), nat_map)
        o_spec = pl.BlockSpec((TM_PROJ, PROJ_W), nat_map)
        shape, dtypes = (n_rows, PROJ_W), ((BF16, BF16) if kind == "qk" else (BF16, F32))
    table_specs = [pl.BlockSpec((TM_PROJ, LANES), lambda i: (i % nt, 0)) for _ in tables]
    return pl.pallas_call(
        functools.partial(_proj_kernel, kind=kind),
        grid=(n_rows // TM_PROJ,),
        in_specs=[
            x_spec,
            pl.BlockSpec((None, D_MODEL, 2 * PROJ_W), lambda i: (layer, 0, pair)),
        ] + table_specs,
        out_specs=[o_spec] * 2,
        out_shape=[jax.ShapeDtypeStruct(shape, dt) for dt in dtypes],
        compiler_params=pltpu.CompilerParams(
            dimension_semantics=("parallel",), vmem_limit_bytes=VMEM_LIMIT),
        name="proj_" + kind,
    )(x_arg, w_bf16, *tables)


def _ssm_state_kernel(u_ref, pwr_ref, bbt_ref, z_ref, h_ref, bk_ref, zs_ref, hs_ref):
    n = pl.program_id(1)

    @pl.when(n == 0)
    def _():
        for s in range(CHUNK):
            z_ref[:, s * LANES:(s + 1) * LANES] = (
                u_ref[:, s].reshape(CHUNK_ROWS, LANES).astype(BF16))

    row = lax.broadcasted_iota(jnp.int32, (LANES, LANES), 0)
    lane = lax.broadcasted_iota(jnp.int32, (LANES, LANES), 1)
    own = (row // SSM_GROUP) == (n * GROUPS_PER_TILE + lane // SSM_STATE)
    for d in range(2):
        b_re = jnp.where(own, bbt_ref[0, d], 0.0)
        b_im = jnp.where(own, bbt_ref[1, d], 0.0)
        for s in range(CHUNK):
            e = CHUNK - 1 - s if d == 0 else s
            p_re = pwr_ref[0, d, e:e + 1, :]
            p_im = pwr_ref[1, d, e:e + 1, :]
            c0 = d * 2 * LANES
            bk_ref[s * LANES:(s + 1) * LANES, c0:c0 + LANES] = (
                p_re * b_re - p_im * b_im).astype(BF16)
            bk_ref[s * LANES:(s + 1) * LANES, c0 + LANES:c0 + 2 * LANES] = (
                p_re * b_im + p_im * b_re).astype(BF16)

    zs_ref[...] = jnp.dot(z_ref[...], bk_ref[...], preferred_element_type=F32)

    decay = lambda ri, d: jnp.broadcast_to(pwr_ref[ri, d, CHUNK:CHUNK + 1, :], (BATCH, LANES))
    fa_re, fa_im = decay(0, 0), decay(1, 0)
    ba_re, ba_im = decay(0, 1), decay(1, 1)

    def body(c, carry):
        f_re, f_im, b_re, b_im = carry
        rf = pl.multiple_of(c * BATCH, SUBLANES)
        rb = pl.multiple_of((N_CHUNKS - 1 - c) * BATCH, SUBLANES)
        hs_ref[pl.ds(rf, BATCH), 0:LANES] = f_re
        hs_ref[pl.ds(rf, BATCH), LANES:2 * LANES] = f_im
        hs_ref[pl.ds(rb, BATCH), 2 * LANES:3 * LANES] = b_re
        hs_ref[pl.ds(rb, BATCH), 3 * LANES:4 * LANES] = b_im
        zf_re = zs_ref[pl.ds(rf, BATCH), 0:LANES]
        zf_im = zs_ref[pl.ds(rf, BATCH), LANES:2 * LANES]
        zb_re = zs_ref[pl.ds(rb, BATCH), 2 * LANES:3 * LANES]
        zb_im = zs_ref[pl.ds(rb, BATCH), 3 * LANES:4 * LANES]
        return (fa_re * f_re - fa_im * f_im + zf_re, fa_re * f_im + fa_im * f_re + zf_im,
                ba_re * b_re - ba_im * b_im + zb_re, ba_re * b_im + ba_im * b_re + zb_im)

    zero = jnp.zeros((BATCH, LANES), F32)
    lax.fori_loop(0, N_CHUNKS, body, (zero, zero, zero, zero))
    h_ref[...] = hs_ref[...].astype(BF16)


def _ssm_state(u5, pwr, bbt):
    tile_map = lambda g, n: (g, n, 0, 0, 0, 0)
    return pl.pallas_call(
        _ssm_state_kernel,
        grid=(N_LANE_BLOCKS, N_STATE_TILES),
        in_specs=[
            pl.BlockSpec((None, N_CHUNKS, CHUNK, BATCH, LANES), lambda g, n: (g, 0, 0, 0, 0)),
            pl.BlockSpec((None, None, 2, 2, CHUNK + 1, LANES), tile_map),
            pl.BlockSpec((None, None, 2, 2, LANES, LANES), tile_map),
        ],
        out_specs=[
            pl.BlockSpec((None, CHUNK_ROWS, ROW_W), lambda g, n: (g, 0, 0)),
            pl.BlockSpec((None, CHUNK_ROWS, STATE_TILE), lambda g, n: (g, 0, n)),
        ],
        out_shape=[
            jax.ShapeDtypeStruct((N_LANE_BLOCKS, CHUNK_ROWS, ROW_W), BF16),
            jax.ShapeDtypeStruct((N_LANE_BLOCKS, CHUNK_ROWS, STATE_W), BF16),
        ],
        scratch_shapes=[pltpu.VMEM((ROW_W, STATE_TILE), BF16),
                        pltpu.VMEM((CHUNK_ROWS, STATE_TILE), F32),
                        pltpu.VMEM((CHUNK_ROWS, STATE_TILE), F32)],
        compiler_params=pltpu.CompilerParams(
            dimension_semantics=("arbitrary", "arbitrary"),
            vmem_limit_bytes=VMEM_LIMIT),
        name="ssm_state",
    )(u5, pwr, bbt)


def _ssm_out_kernel(z_ref, h_ref, tap_ref, rep_ref, cdup_ref, pwt_ref, ue_ref, d_ref, y_ref,
                    blk_ref, wt_ref, wc_ref):
    g = pl.program_id(0)
    n = pl.program_id(1)

    @pl.when(n == 0)
    def _():
        x = jnp.dot(tap_ref[...], rep_ref[...], preferred_element_type=F32)
        row = lax.broadcasted_iota(jnp.int32, x.shape, 0)
        lane = lax.broadcasted_iota(jnp.int32, x.shape, 1)
        own = ((row % LANES) // SSM_GROUP) == (lane // SSM_GROUP)
        blk_ref[...] = jnp.where(own, x, 0.0).astype(BF16)

    @pl.when((g == 0) & (n == 0))
    def _():
        wc_ref[...] = jnp.zeros_like(wc_ref)

    for tt in range(SLABS_PER_TILE):
        t = n * SLABS_PER_TILE + tt
        for s in range(CHUNK):
            off = pl.multiple_of((t - s + CHUNK - 1) * LANES, LANES)
            wt_ref[tt * LANES:(tt + 1) * LANES, s * LANES:(s + 1) * LANES] = (
                blk_ref[pl.ds(off, LANES), :])

    half = lax.broadcasted_iota(jnp.int32, (SSM_GROUP, LANES), 1) // SSM_STATE
    for tt in range(SLABS_PER_TILE):
        for h in range(GROUPS_PER_BLOCK):
            own = half == (h % GROUPS_PER_TILE)
            r0 = tt * LANES + h * SSM_GROUP
            for d in range(2):
                c_re = cdup_ref[0, d, h * SSM_GROUP:(h + 1) * SSM_GROUP, :]
                c_im = cdup_ref[1, d, h * SSM_GROUP:(h + 1) * SSM_GROUP, :]
                p_re = pwt_ref[tt, 0, d, h:h + 1, :]
                p_im = pwt_ref[tt, 1, d, h:h + 1, :]
                col = (h // GROUPS_PER_TILE) * STATE_TILE + d * 2 * LANES
                wc_ref[r0:r0 + SSM_GROUP, col:col + LANES] = jnp.where(
                    own, c_re * p_re - c_im * p_im, 0.0).astype(BF16)
                wc_ref[r0:r0 + SSM_GROUP, col + LANES:col + 2 * LANES] = jnp.where(
                    own, -(c_re * p_im + c_im * p_re), 0.0).astype(BF16)

    d_skip = d_ref[...]
    slabs_per_slice = PROJ_SLICE // LANES
    for half in range(OUT_TILE // PROJ_SLICE):
        rows = slice(half * PROJ_SLICE, (half + 1) * PROJ_SLICE)
        y = (lax.dot_general(z_ref[...], wt_ref[rows, :], _NT, preferred_element_type=F32)
             + lax.dot_general(h_ref[...], wc_ref[rows, :], _NT, preferred_element_type=F32))
        for j in range(slabs_per_slice):
            tt = half * slabs_per_slice + j
            yt = y[:, j * LANES:(j + 1) * LANES].reshape(N_CHUNKS, BATCH, LANES)
            y_ref[:, tt] = jax.nn.gelu(yt + d_skip * ue_ref[:, tt])


def _ssm_out(zflat, hstate, tapr, rep, cdup, pwt, u4, d_b):
    assert SSM_GROUP == BF16_ROWS
    return pl.pallas_call(
        _ssm_out_kernel,
        grid=(N_LANE_BLOCKS, CHUNK // SLABS_PER_TILE),
        in_specs=[
            pl.BlockSpec((None, CHUNK_ROWS, ROW_W), lambda g, n: (g, 0, 0)),
            pl.BlockSpec((None, CHUNK_ROWS, STATE_W), lambda g, n: (g, 0, 0)),
            pl.BlockSpec((None, N_LAGS * LANES, SSM_GROUP), lambda g, n: (g, 0, 0)),
            pl.BlockSpec((SSM_GROUP, LANES), lambda g, n: (0, 0)),
            pl.BlockSpec((None, 2, 2, LANES, LANES), lambda g, n: (g, 0, 0, 0, 0)),
            pl.BlockSpec((None, SLABS_PER_TILE, 2, 2, GROUPS_PER_BLOCK, LANES),
                         lambda g, n: (g, n, 0, 0, 0, 0)),
            pl.BlockSpec((None, N_CHUNKS, SLABS_PER_TILE, BATCH, LANES),
                         lambda g, n: (g, 0, n, 0, 0)),
            pl.BlockSpec((None, SUBLANES, LANES), lambda g, n: (g, 0, 0)),
        ],
        out_specs=pl.BlockSpec((None, N_CHUNKS, SLABS_PER_TILE, BATCH, LANES),
                               lambda g, n: (g, 0, n, 0, 0)),
        out_shape=jax.ShapeDtypeStruct((N_LANE_BLOCKS, N_CHUNKS, CHUNK, BATCH, LANES), F32),
        scratch_shapes=[pltpu.VMEM((N_LAGS * LANES, LANES), BF16),
                        pltpu.VMEM((OUT_TILE, ROW_W), BF16),
                        pltpu.VMEM((OUT_TILE, STATE_W), BF16)],
        compiler_params=pltpu.CompilerParams(
            dimension_semantics=("arbitrary", "arbitrary"),
            vmem_limit_bytes=VMEM_LIMIT),
        name="ssm_out",
    )(zflat, hstate, tapr, rep, cdup, pwt, u4, d_b)


def _glu_kernel(y_ref, g_ref, w_ref, b_ref, o_ref, ys_ref, gs_ref):
    t_rows = TM_GLU // BATCH
    for j in range(N_LANE_BLOCKS):
        for b in range(BATCH):
            rows = slice(b * t_rows, (b + 1) * t_rows)
            cols = slice(j * LANES, (j + 1) * LANES)
            ys_ref[rows, cols] = y_ref[j, pl.ds(b, t_rows, stride=BATCH), :]
            gs_ref[rows, cols] = g_ref[j, pl.ds(b, t_rows, stride=BATCH), :]
    y = ys_ref[...]
    z = jnp.dot(y.astype(BF16), w_ref[...], preferred_element_type=F32) + b_ref[...]
    g = gs_ref[...]
    out = (y * jax.nn.sigmoid(z) * (g * jax.nn.sigmoid(g))).astype(BF16)
    o_ref[...] = out.reshape(BATCH, t_rows, D_SSM)


def _glu(y_slabs, g_slabs, w_bf16, b_row):
    slab_spec = pl.BlockSpec((N_LANE_BLOCKS, TM_GLU, LANES), lambda i: (0, i, 0))
    return pl.pallas_call(
        _glu_kernel,
        grid=(BATCH * SEQ // TM_GLU,),
        in_specs=[
            slab_spec,
            slab_spec,
            pl.BlockSpec((D_SSM, D_SSM), lambda i: (0, 0)),
            pl.BlockSpec((1, D_SSM), lambda i: (0, 0)),
        ],
        out_specs=pl.BlockSpec((BATCH, TM_GLU // BATCH, D_SSM), lambda i: (0, i, 0)),
        out_shape=jax.ShapeDtypeStruct((BATCH, SEQ, D_SSM), BF16),
        scratch_shapes=[pltpu.VMEM((TM_GLU, D_SSM), F32), pltpu.VMEM((TM_GLU, D_SSM), F32)],
        compiler_params=pltpu.CompilerParams(
            dimension_semantics=("parallel",), vmem_limit_bytes=VMEM_LIMIT),
        name="glu",
    )(y_slabs, g_slabs, w_bf16, b_row)


def _attn_kernel(lam_ref, q_ref, k_ref, v_ref, ga_ref, ng_ref, o_ref, *, post_scale):
    lam = lam_ref[0]
    lane = lax.broadcasted_iota(jnp.int32, (TQ, V_DIM), 1)
    zero = jnp.zeros((TQ, V_DIM), BF16)
    tiles = [(hh, r) for hh in range(HEADS_PER_STEP) for r in range(SEQ // TQ)]

    def scores(hh, r):
        cols = slice(hh * V_DIM, (hh + 1) * V_DIM)
        q = q_ref[r * TQ:(r + 1) * TQ, cols]
        k = k_ref[:, cols]
        return (lax.dot_general(jnp.where(lane < HEAD_DIM, q, zero), k, _NT,
                                preferred_element_type=F32),
                lax.dot_general(jnp.where(lane >= HEAD_DIM, q, zero), k, _NT,
                                preferred_element_type=F32))

    def exp_sum(s):
        e = jnp.exp2(s - jnp.max(s, axis=1, keepdims=True))
        return e, jnp.sum(e, axis=1, keepdims=True)

    s_next = scores(*tiles[0])
    for t, (hh, r) in enumerate(tiles):
        s1, s2 = s_next
        if t + 1 < len(tiles):
            s_next = scores(*tiles[t + 1])
        rows = slice(r * TQ, (r + 1) * TQ)
        cols = slice(hh * V_DIM, (hh + 1) * V_DIM)
        e1, l1 = exp_sum(s1)
        e2, l2 = exp_sum(s2)
        w = (e1 - (lam * l1 / l2) * e2).astype(BF16)
        o = jnp.dot(w, v_ref[:, cols], preferred_element_type=F32) / l1
        ms = jnp.mean(o * o, axis=1, keepdims=True)
        o = o * lax.rsqrt(ms + RMS_EPS) * ng_ref[...] * post_scale
        ga = ga_ref[rows, cols]
        o_ref[rows, cols] = (o * (ga * jax.nn.sigmoid(ga))).astype(BF16)


def _attention(lam, q, k, v, ga, ng_row, *, post_scale):
    blk = pl.BlockSpec((SEQ, HEADS_PER_STEP * V_DIM), lambda b, h: (b, h))
    return pl.pallas_call(
        functools.partial(_attn_kernel, post_scale=post_scale),
        grid=(BATCH, N_HEADS // HEADS_PER_STEP),
        in_specs=[
            pl.BlockSpec(memory_space=pltpu.SMEM),
            blk,
            blk,
            blk,
            blk,
            pl.BlockSpec((1, V_DIM), lambda b, h: (0, 0)),
        ],
        out_specs=blk,
        out_shape=jax.ShapeDtypeStruct((BATCH * SEQ, D_ATTN), BF16),
        compiler_params=pltpu.CompilerParams(
            dimension_semantics=("parallel", "parallel"),
            vmem_limit_bytes=VMEM_LIMIT),
        name="diff_attn",
    )(lam, q, k, v, ga, ng_row)


def _out_proj_kernel(ms_ref, ma_ref, w1_ref, w2_ref, x_ref, eg_ref, eb_ref, g_ref, b_ref,
                     o_ref, *maybe_ob_ref, pre_ln):
    for n in range(D_MODEL // PROJ_SLICE):
        cols = slice(n * PROJ_SLICE, (n + 1) * PROJ_SLICE)
        o_ref[:, cols] = (
            jnp.dot(ms_ref[...], w1_ref[:, cols], preferred_element_type=F32)
            + jnp.dot(ma_ref[...], w2_ref[:, cols], preferred_element_type=F32))
    x = x_ref[...]
    if pre_ln:
        x = _layer_norm(x, eg_ref[...], eb_ref[...])
    y = _layer_norm(DEEPNORM_ALPHA * x + o_ref[...], g_ref[...], b_ref[...])
    o_ref[...] = y
    for ob_ref in maybe_ob_ref:
        ob_ref[...] = y.astype(BF16)


def _out_proj(mix_ssm, mix_attn, w_bf16, layer, x2d, eg, eb, g, b, *, pre_ln, emit_bf16):
    n_rows = BATCH * SEQ
    row = lambda i: (0, 0)
    nat = lambda i: (i, 0)
    out_specs = [pl.BlockSpec((TM_OUT, D_MODEL), nat)]
    out_shape = [jax.ShapeDtypeStruct((n_rows, D_MODEL), F32)]
    if emit_bf16:
        out_specs.append(pl.BlockSpec((TM_OUT, D_MODEL), nat))
        out_shape.append(jax.ShapeDtypeStruct((n_rows, D_MODEL), BF16))
    return pl.pallas_call(
        functools.partial(_out_proj_kernel, pre_ln=pre_ln),
        grid=(n_rows // TM_OUT,),
        in_specs=[
            pl.BlockSpec((TM_OUT, D_SSM), nat),
            pl.BlockSpec((TM_OUT, D_ATTN), nat),
            pl.BlockSpec((None, D_SSM, D_MODEL), lambda i: (layer, 0, 0)),
            pl.BlockSpec((None, D_ATTN, D_MODEL), lambda i: (layer, 1, 0)),
            pl.BlockSpec((TM_OUT, D_MODEL), nat),
            pl.BlockSpec((1, D_MODEL), row),
            pl.BlockSpec((1, D_MODEL), row),
            pl.BlockSpec((1, D_MODEL), row),
            pl.BlockSpec((1, D_MODEL), row),
        ],
        out_specs=out_specs,
        out_shape=out_shape,
        compiler_params=pltpu.CompilerParams(
            dimension_semantics=("parallel",), vmem_limit_bytes=VMEM_LIMIT),
        name="out_proj",
    )(mix_ssm, mix_attn, w_bf16, w_bf16, x2d, eg, eb, g, b)


def _rotary_tables():
    pos = jnp.arange(SEQ, dtype=F32)
    inv_freq = ROPE_THETA ** (-jnp.arange(0, ROT_DIM, 2, dtype=F32) / ROT_DIM)
    ang = pos[:, None] * inv_freq[None, :]
    cos = jnp.cos(ang)
    sin = jnp.sin(ang)
    ones = jnp.ones((SEQ, HEAD_DIM - ROT_DIM), F32)
    zeros = jnp.zeros((SEQ, HEAD_DIM - ROT_DIM), F32)
    z8 = jnp.zeros((SEQ, ROT_HALF), F32)
    reps = LANES // HEAD_DIM
    cos_t = jnp.tile(jnp.concatenate([cos, cos, ones], axis=1), (1, reps))
    sin_a = jnp.tile(jnp.concatenate([-sin, z8, zeros], axis=1), (1, reps))
    sin_b = jnp.tile(jnp.concatenate([z8, sin, zeros], axis=1), (1, reps))
    return cos_t, sin_a, sin_b


def _ssm_tables(lam_re, lam_im, log_step, b_re, b_im, c_re, c_im):
    G, P, C, L = N_SSM_GROUPS, SSM_STATE, SSM_GROUP, CHUNK
    GB, NT, GT = N_LANE_BLOCKS, N_STATE_TILES, GROUPS_PER_TILE
    hi = lax.Precision.HIGHEST
    step = jnp.exp(log_step)[..., None]
    zr = lam_re * step
    zi = lam_im * step
    kpow = jnp.arange(L + 1, dtype=F32)[:, None, None, None]
    mag = jnp.exp(kpow * zr[None])
    pw = jnp.stack([mag * jnp.cos(kpow * zi[None]),
                    mag * jnp.sin(kpow * zi[None])])
    nr = pw[0, 1] - 1.0
    ni = pw[1, 1]
    den = lam_re * lam_re + lam_im * lam_im
    coef_re = (nr * lam_re + ni * lam_im) / den
    coef_im = (ni * lam_re - nr * lam_im) / den
    bb = jnp.stack([coef_re[..., None] * b_re - coef_im[..., None] * b_im,
                    coef_re[..., None] * b_im + coef_im[..., None] * b_re])

    cw_re = c_re[None] * pw[0, :L, :, :, None, :] - c_im[None] * pw[1, :L, :, :, None, :]
    cw_im = c_re[None] * pw[1, :L, :, :, None, :] + c_im[None] * pw[0, :L, :, :, None, :]
    taps = (jnp.einsum('kdgop,dgpi->kdgio', cw_re, bb[0], precision=hi)
            - jnp.einsum('kdgop,dgpi->kdgio', cw_im, bb[1], precision=hi))
    tap_lag = jnp.concatenate([taps[:0:-1, 1], (taps[0, 0] + taps[0, 1])[None], taps[1:, 0]])
    tapr = tap_lag.transpose(0, 1, 3, 2).reshape(N_LAGS, GB, LANES, C)
    tapr = tapr.transpose(1, 0, 2, 3).reshape(GB, N_LAGS * LANES, C).astype(BF16)

    pwr = pw.reshape(2, L + 1, 2, GB, NT, GT * P).transpose(3, 4, 0, 2, 1, 5)
    bbt = bb.reshape(2, 2, GB, NT, GT, P, C).transpose(2, 3, 0, 1, 6, 4, 5)
    bbt = jnp.tile(bbt.reshape(GB, NT, 2, 2, C, LANES), (1, 1, 1, 1, GROUPS_PER_BLOCK, 1))

    cc = jnp.stack([c_re, c_im]).reshape(2, 2, GB, LANES, P).transpose(2, 0, 1, 3, 4)
    cdup = jnp.concatenate([cc, cc], axis=-1)
    ar_l = jnp.arange(L)
    pt = jnp.stack([pw[:, ar_l + 1, 0], pw[:, L - ar_l, 1]], axis=2)
    pt = pt.reshape(2, L, 2, GB, GROUPS_PER_BLOCK, P).transpose(3, 1, 0, 2, 4, 5)
    pwt = jnp.concatenate([pt, pt], axis=-1)
    return tapr, pwr, bbt, cdup, pwt


def kernel(x, ln_emb_g, ln_emb_b, w_in, ssm_lam_re, ssm_lam_im, ssm_log_step, ssm_b_re,
           ssm_b_im, ssm_c_re, ssm_c_im, ssm_d, w_glu, b_glu, lambda_q1, lambda_k1,
           lambda_q2, lambda_k2, attn_norm_g, w_out, ln_g, ln_b):
    assert x.shape == (BATCH, SEQ, D_MODEL) and x.dtype == F32
    cos_t, sin_a, sin_b = _rotary_tables()
    eg = ln_emb_g.reshape(1, D_MODEL)
    eb = ln_emb_b.reshape(1, D_MODEL)
    x2d = x.reshape(BATCH * SEQ, D_MODEL)
    rep = (jnp.arange(LANES)[None, :] % SSM_GROUP == jnp.arange(SSM_GROUP)[:, None]).astype(BF16)

    w_in_bf16 = w_in.astype(BF16)
    w_out_bf16 = w_out.astype(BF16)
    xb = _emb_ln(x2d, eg, eb)
    for l in range(DEPTH):
        pre_ln = l == 0
        u_tb, gs_tb = _proj_pair(xb, w_in_bf16, l, 0, "ssm")
        q, k = _proj_pair(xb, w_in_bf16, l, 1, "qk", (cos_t, sin_a, sin_b))
        v, ga = _proj_pair(xb, w_in_bf16, l, 2, "vg")

        tapr, pwr, bbt, cdup, pwt = _ssm_tables(
            ssm_lam_re[l], ssm_lam_im[l], ssm_log_step[l], ssm_b_re[l], ssm_b_im[l],
            ssm_c_re[l], ssm_c_im[l])
        u5 = u_tb.reshape(N_LANE_BLOCKS, N_CHUNKS, CHUNK, BATCH, LANES)
        zflat, hstate = _ssm_state(u5, pwr, bbt)
        d_b = jnp.broadcast_to(ssm_d[l].reshape(N_LANE_BLOCKS, 1, LANES),
                               (N_LANE_BLOCKS, SUBLANES, LANES))
        y5 = _ssm_out(zflat, hstate, tapr, rep, cdup, pwt, u5, d_b)
        mix_ssm = _glu(y5.reshape(N_LANE_BLOCKS, SEQ * BATCH, LANES), gs_tb,
                       w_glu[l].astype(BF16), b_glu[l].reshape(1, D_SSM))

        lambda_init = 0.8 - 0.6 * math.exp(-0.3 * l)
        lam = (jnp.exp(jnp.sum(lambda_q1[l] * lambda_k1[l]))
               - jnp.exp(jnp.sum(lambda_q2[l] * lambda_k2[l])) + lambda_init)
        mix_attn = _attention(lam.reshape(1).astype(F32), q, k, v, ga,
                              attn_norm_g[l].reshape(1, V_DIM), post_scale=1.0 - lambda_init)

        outs = _out_proj(mix_ssm.reshape(BATCH * SEQ, D_SSM), mix_attn, w_out_bf16, l,
                         x2d, eg, eb, ln_g[l].reshape(1, D_MODEL), ln_b[l].reshape(1, D_MODEL),
                         pre_ln=pre_ln, emit_bf16=l + 1 < DEPTH)
        x2d = outs[0]
        if l + 1 < DEPTH:
            xb = outs[1]
    return x2d.reshape(BATCH, SEQ, D_MODEL)
```

```python
import functools
import math

import jax
import jax.numpy as jnp
from jax import lax
from jax.experimental import pallas as pl
from jax.experimental.pallas import tpu as pltpu

F32 = jnp.float32
BF16 = jnp.bfloat16

D_MODEL = 2048
BATCH = 8
SEQ = 2048
DEPTH = 2
D_SSM = 1024
D_ATTN = 1024
SSM_GROUP = 16
N_SSM_GROUPS = D_SSM // SSM_GROUP
SSM_STATE = 64
HEAD_DIM = 64
V_DIM = 2 * HEAD_DIM
N_HEADS = D_ATTN // V_DIM
ROT_DIM = HEAD_DIM // 4
ROT_HALF = ROT_DIM // 2
ROPE_THETA = 500000.0
LN_EPS = 1e-5
RMS_EPS = 1e-5
DEEPNORM_ALPHA = (2.0 * DEPTH) ** 0.25
PROJ_W = 1024

LANES = 128
SUBLANES = 8
BF16_ROWS = 16

CHUNK = 16
N_CHUNKS = SEQ // CHUNK
N_LAGS = 2 * CHUNK - 1
GROUPS_PER_BLOCK = LANES // SSM_GROUP
N_LANE_BLOCKS = D_SSM // LANES
ROW_W = CHUNK * LANES
GROUPS_PER_TILE = LANES // SSM_STATE
N_STATE_TILES = GROUPS_PER_BLOCK // GROUPS_PER_TILE
STATE_TILE = 2 * 2 * LANES
STATE_W = N_STATE_TILES * STATE_TILE
SLABS_PER_TILE = 4
OUT_TILE = SLABS_PER_TILE * LANES
CHUNK_ROWS = N_CHUNKS * BATCH

TM_PROJ = 1024
PROJ_SLICE = 256
TM_GLU = 512
TM_OUT = 512
TQ = 512
HEADS_PER_STEP = 1
Q_SCALE = HEAD_DIM ** -0.5 * math.log2(math.e)

VMEM_LIMIT = 56 * 1024 * 1024

_NT = (((1,), (1,)), ((), ()))


def _layer_norm(x, g, b):
    mu = jnp.mean(x, axis=-1, keepdims=True)
    xc = x - mu
    var = jnp.mean(xc * xc, axis=-1, keepdims=True)
    return xc * lax.rsqrt(var + LN_EPS) * g + b


def _emb_ln_kernel(x_ref, g_ref, b_ref, o_ref):
    o_ref[...] = _layer_norm(x_ref[...], g_ref[...], b_ref[...]).astype(BF16)


def _emb_ln(x2d, eg, eb):
    n_rows = BATCH * SEQ
    return pl.pallas_call(
        _emb_ln_kernel,
        grid=(n_rows // TM_PROJ,),
        in_specs=[
            pl.BlockSpec((TM_PROJ, D_MODEL), lambda i: (i, 0)),
            pl.BlockSpec((1, D_MODEL), lambda i: (0, 0)),
            pl.BlockSpec((1, D_MODEL), lambda i: (0, 0)),
        ],
        out_specs=pl.BlockSpec((TM_PROJ, D_MODEL), lambda i: (i, 0)),
        out_shape=jax.ShapeDtypeStruct((n_rows, D_MODEL), BF16),
        compiler_params=pltpu.CompilerParams(
            dimension_semantics=("parallel",), vmem_limit_bytes=VMEM_LIMIT),
        name="emb_ln",
    )(x2d, eg, eb)


def _proj_kernel(x_ref, w_ref, *refs, kind):
    if kind == "qk":
        cos_ref, sa_ref, sb_ref, oa_ref, ob_ref = refs
    else:
        oa_ref, ob_ref = refs
    x = x_ref[...].reshape(TM_PROJ, D_MODEL)
    slices_per_piece = PROJ_W // PROJ_SLICE
    for n in range(2 * slices_per_piece):
        acc = jnp.dot(x, w_ref[:, n * PROJ_SLICE:(n + 1) * PROJ_SLICE],
                      preferred_element_type=F32)
        first = n < slices_per_piece
        o_ref = oa_ref if first else ob_ref
        col = (n % slices_per_piece) * PROJ_SLICE
        if kind == "ssm" and first:
            t_rows = TM_PROJ // BATCH
            for cb in range(PROJ_SLICE // LANES):
                for b in range(BATCH):
                    o_ref[col // LANES + cb, pl.ds(b, t_rows, stride=BATCH), :] = (
                        acc[b * t_rows:(b + 1) * t_rows, cb * LANES:(cb + 1) * LANES])
        elif kind == "ssm":
            o_ref[:, :, col:col + PROJ_SLICE] = acc.reshape(BATCH, TM_PROJ // BATCH, PROJ_SLICE)
        elif kind == "qk":
            for cb in range(PROJ_SLICE // LANES):
                t = acc[:, cb * LANES:(cb + 1) * LANES]
                r = (t * cos_ref[...] + pltpu.roll(t, LANES - ROT_HALF, 1) * sa_ref[...]
                     + pltpu.roll(t, ROT_HALF, 1) * sb_ref[...])
                if first:
                    r = r * Q_SCALE
                o_ref[:, col + cb * LANES:col + (cb + 1) * LANES] = r.astype(BF16)
        else:
            o_ref[:, col:col + PROJ_SLICE] = acc.astype(o_ref.dtype)


def _proj_pair(xb, w_bf16, layer, pair, kind, tables=()):
    nt = SEQ // TM_PROJ
    n_rows = BATCH * SEQ
    nat_map = lambda i: (i, 0)
    if kind == "ssm":
        t_rows = TM_PROJ // BATCH
        x_arg = xb.reshape(BATCH, SEQ, D_MODEL)
        x_spec = pl.BlockSpec((BATCH, t_rows, D_MODEL), lambda i: (0, i, 0))
        out_specs = [pl.BlockSpec((N_LANE_BLOCKS, TM_PROJ, LANES), lambda i: (0, i, 0)),
                     pl.BlockSpec((BATCH, t_rows, PROJ_W), lambda i: (0, i, 0))]
        out_shape = [jax.ShapeDtypeStruct((N_LANE_BLOCKS, n_rows, LANES), F32),
                     jax.ShapeDtypeStruct((BATCH, SEQ, PROJ_W), F32)]
    else:
        x_arg = xb
        x_spec = pl.BlockSpec((TM_PROJ, D_MODEL), nat_map)
        out_specs = [pl.BlockSpec((TM_PROJ, PROJ_W), nat_map)] * 2
        out_shape = [jax.ShapeDtypeStruct((n_rows, PROJ_W), dt)
                     for dt in ((BF16, BF16) if kind == "qk" else (BF16, F32))]
    table_specs = [pl.BlockSpec((TM_PROJ, LANES), lambda i: (i % nt, 0)) for _ in tables]
    return pl.pallas_call(
        functools.partial(_proj_kernel, kind=kind),
        grid=(n_rows // TM_PROJ,),
        in_specs=[
            x_spec,
            pl.BlockSpec((None, D_MODEL, 2 * PROJ_W), lambda i: (layer, 0, pair)),
        ] + table_specs,
        out_specs=out_specs,
        out_shape=out_shape,
        compiler_params=pltpu.CompilerParams(
            dimension_semantics=("parallel",), vmem_limit_bytes=VMEM_LIMIT),
        name="proj_" + kind,
    )(x_arg, w_bf16, *tables)


def _ssm_state_kernel(u_ref, pwr_ref, bbt_ref, z_ref, h_ref, bk_ref, zs_ref, hs_ref):
    n = pl.program_id(1)

    @pl.when(n == 0)
    def _():
        for s in range(CHUNK):
            z_ref[:, s * LANES:(s + 1) * LANES] = (
                u_ref[:, s].reshape(CHUNK_ROWS, LANES).astype(BF16))

    row = lax.broadcasted_iota(jnp.int32, (LANES, LANES), 0)
    lane = lax.broadcasted_iota(jnp.int32, (LANES, LANES), 1)
    own = (row // SSM_GROUP) == (n * GROUPS_PER_TILE + lane // SSM_STATE)
    for d in range(2):
        b_re = jnp.where(own, bbt_ref[0, d], 0.0)
        b_im = jnp.where(own, bbt_ref[1, d], 0.0)
        for s in range(CHUNK):
            e = CHUNK - 1 - s if d == 0 else s
            p_re = pwr_ref[0, d, e:e + 1, :]
            p_im = pwr_ref[1, d, e:e + 1, :]
            c0 = d * 2 * LANES
            bk_ref[s * LANES:(s + 1) * LANES, c0:c0 + LANES] = (
                p_re * b_re - p_im * b_im).astype(BF16)
            bk_ref[s * LANES:(s + 1) * LANES, c0 + LANES:c0 + 2 * LANES] = (
                p_re * b_im + p_im * b_re).astype(BF16)

    zs_ref[...] = jnp.dot(z_ref[...], bk_ref[...], preferred_element_type=F32)

    decay = lambda ri, d: jnp.broadcast_to(pwr_ref[ri, d, CHUNK:CHUNK + 1, :], (BATCH, LANES))
    fa_re, fa_im = decay(0, 0), decay(1, 0)
    ba_re, ba_im = decay(0, 1), decay(1, 1)

    def body(c, carry):
        f_re, f_im, b_re, b_im = carry
        rf = pl.multiple_of(c * BATCH, SUBLANES)
        rb = pl.multiple_of((N_CHUNKS - 1 - c) * BATCH, SUBLANES)
        hs_ref[pl.ds(rf, BATCH), 0:LANES] = f_re
        hs_ref[pl.ds(rf, BATCH), LANES:2 * LANES] = f_im
        hs_ref[pl.ds(rb, BATCH), 2 * LANES:3 * LANES] = b_re
        hs_ref[pl.ds(rb, BATCH), 3 * LANES:4 * LANES] = b_im
        zf_re = zs_ref[pl.ds(rf, BATCH), 0:LANES]
        zf_im = zs_ref[pl.ds(rf, BATCH), LANES:2 * LANES]
        zb_re = zs_ref[pl.ds(rb, BATCH), 2 * LANES:3 * LANES]
        zb_im = zs_ref[pl.ds(rb, BATCH), 3 * LANES:4 * LANES]
        return (fa_re * f_re - fa_im * f_im + zf_re, fa_re * f_im + fa_im * f_re + zf_im,
                ba_re * b_re - ba_im * b_im + zb_re, ba_re * b_im + ba_im * b_re + zb_im)

    zero = jnp.zeros((BATCH, LANES), F32)
    lax.fori_loop(0, N_CHUNKS, body, (zero, zero, zero, zero))
    h_ref[...] = hs_ref[...].astype(BF16)


def _ssm_state(u5, pwr, bbt):
    tile_map = lambda g, n: (g, n, 0, 0, 0, 0)
    return pl.pallas_call(
        _ssm_state_kernel,
        grid=(N_LANE_BLOCKS, N_STATE_TILES),
        in_specs=[
            pl.BlockSpec((None, N_CHUNKS, CHUNK, BATCH, LANES), lambda g, n: (g, 0, 0, 0, 0)),
            pl.BlockSpec((None, None, 2, 2, CHUNK + 1, LANES), tile_map),
            pl.BlockSpec((None, None, 2, 2, LANES, LANES), tile_map),
        ],
        out_specs=[
            pl.BlockSpec((None, CHUNK_ROWS, ROW_W), lambda g, n: (g, 0, 0)),
            pl.BlockSpec((None, CHUNK_ROWS, STATE_TILE), lambda g, n: (g, 0, n)),
        ],
        out_shape=[
            jax.ShapeDtypeStruct((N_LANE_BLOCKS, CHUNK_ROWS, ROW_W), BF16),
            jax.ShapeDtypeStruct((N_LANE_BLOCKS, CHUNK_ROWS, STATE_W), BF16),
        ],
        scratch_shapes=[pltpu.VMEM((ROW_W, STATE_TILE), BF16),
                        pltpu.VMEM((CHUNK_ROWS, STATE_TILE), F32),
                        pltpu.VMEM((CHUNK_ROWS, STATE_TILE), F32)],
        compiler_params=pltpu.CompilerParams(
            dimension_semantics=("arbitrary", "arbitrary"),
            vmem_limit_bytes=VMEM_LIMIT),
        name="ssm_state",
    )(u5, pwr, bbt)


def _ssm_out_kernel(z_ref, h_ref, tap_ref, rep_ref, cdup_ref, pwt_ref, ue_ref, d_ref, y_ref,
                    blk_ref, wt_ref, wc_ref):
    g = pl.program_id(0)
    n = pl.program_id(1)

    @pl.when(n == 0)
    def _():
        x = jnp.dot(tap_ref[...], rep_ref[...], preferred_element_type=F32)
        row = lax.broadcasted_iota(jnp.int32, x.shape, 0)
        lane = lax.broadcasted_iota(jnp.int32, x.shape, 1)
        own = ((row % LANES) // SSM_GROUP) == (lane // SSM_GROUP)
        blk_ref[...] = jnp.where(own, x, 0.0).astype(BF16)

    @pl.when((g == 0) & (n == 0))
    def _():
        wc_ref[...] = jnp.zeros_like(wc_ref)

    for tt in range(SLABS_PER_TILE):
        t = n * SLABS_PER_TILE + tt
        for s in range(CHUNK):
            off = pl.multiple_of((t - s + CHUNK - 1) * LANES, LANES)
            wt_ref[tt * LANES:(tt + 1) * LANES, s * LANES:(s + 1) * LANES] = (
                blk_ref[pl.ds(off, LANES), :])

    half = lax.broadcasted_iota(jnp.int32, (SSM_GROUP, LANES), 1) // SSM_STATE
    for tt in range(SLABS_PER_TILE):
        for h in range(GROUPS_PER_BLOCK):
            own = half == (h % GROUPS_PER_TILE)
            r0 = tt * LANES + h * SSM_GROUP
            for d in range(2):
                c_re = cdup_ref[0, d, h * SSM_GROUP:(h + 1) * SSM_GROUP, :]
                c_im = cdup_ref[1, d, h * SSM_GROUP:(h + 1) * SSM_GROUP, :]
                p_re = pwt_ref[tt, 0, d, h:h + 1, :]
                p_im = pwt_ref[tt, 1, d, h:h + 1, :]
                col = (h // GROUPS_PER_TILE) * STATE_TILE + d * 2 * LANES
                wc_ref[r0:r0 + SSM_GROUP, col:col + LANES] = jnp.where(
                    own, c_re * p_re - c_im * p_im, 0.0).astype(BF16)
                wc_ref[r0:r0 + SSM_GROUP, col + LANES:col + 2 * LANES] = jnp.where(
                    own, -(c_re * p_im + c_im * p_re), 0.0).astype(BF16)

    d_skip = d_ref[...]
    slabs_per_slice = PROJ_SLICE // LANES
    for half in range(OUT_TILE // PROJ_SLICE):
        rows = slice(half * PROJ_SLICE, (half + 1) * PROJ_SLICE)
        y = (lax.dot_general(z_ref[...], wt_ref[rows, :], _NT, preferred_element_type=F32)
             + lax.dot_general(h_ref[...], wc_ref[rows, :], _NT, preferred_element_type=F32))
        for j in range(slabs_per_slice):
            tt = half * slabs_per_slice + j
            yt = y[:, j * LANES:(j + 1) * LANES].reshape(N_CHUNKS, BATCH, LANES)
            y_ref[:, tt] = jax.nn.gelu(yt + d_skip * ue_ref[:, tt])


def _ssm_out(zflat, hstate, tapr, rep, cdup, pwt, u4, d_b):
    assert SSM_GROUP == BF16_ROWS
    return pl.pallas_call(
        _ssm_out_kernel,
        grid=(N_LANE_BLOCKS, CHUNK // SLABS_PER_TILE),
        in_specs=[
            pl.BlockSpec((None, CHUNK_ROWS, ROW_W), lambda g, n: (g, 0, 0)),
            pl.BlockSpec((None, CHUNK_ROWS, STATE_W), lambda g, n: (g, 0, 0)),
            pl.BlockSpec((None, N_LAGS * LANES, SSM_GROUP), lambda g, n: (g, 0, 0)),
            pl.BlockSpec((SSM_GROUP, LANES), lambda g, n: (0, 0)),
            pl.BlockSpec((None, 2, 2, LANES, LANES), lambda g, n: (g, 0, 0, 0, 0)),
            pl.BlockSpec((None, SLABS_PER_TILE, 2, 2, GROUPS_PER_BLOCK, LANES),
                         lambda g, n: (g, n, 0, 0, 0, 0)),
            pl.BlockSpec((None, N_CHUNKS, SLABS_PER_TILE, BATCH, LANES),
                         lambda g, n: (g, 0, n, 0, 0)),
            pl.BlockSpec((None, SUBLANES, LANES), lambda g, n: (g, 0, 0)),
        ],
        out_specs=pl.BlockSpec((None, N_CHUNKS, SLABS_PER_TILE, BATCH, LANES),
                               lambda g, n: (g, 0, n, 0, 0)),
        out_shape=jax.ShapeDtypeStruct((N_LANE_BLOCKS, N_CHUNKS, CHUNK, BATCH, LANES), F32),
        scratch_shapes=[pltpu.VMEM((N_LAGS * LANES, LANES), BF16),
                        pltpu.VMEM((OUT_TILE, ROW_W), BF16),
                        pltpu.VMEM((OUT_TILE, STATE_W), BF16)],
        compiler_params=pltpu.CompilerParams(
            dimension_semantics=("arbitrary", "arbitrary"),
            vmem_limit_bytes=VMEM_LIMIT),
        name="ssm_out",
    )(zflat, hstate, tapr, rep, cdup, pwt, u4, d_b)


def _glu_kernel(y_ref, g_ref, w_ref, b_ref, o_ref, ys_ref):
    t_rows = TM_GLU // BATCH
    for j in range(N_LANE_BLOCKS):
        for b in range(BATCH):
            ys_ref[b * t_rows:(b + 1) * t_rows, j * LANES:(j + 1) * LANES] = (
                y_ref[j, pl.ds(b, t_rows, stride=BATCH), :])
    y = ys_ref[...]
    z = jnp.dot(y.astype(BF16), w_ref[...], preferred_element_type=F32) + b_ref[...]
    g = g_ref[...].reshape(TM_GLU, D_SSM)
    out = (y * jax.nn.sigmoid(z) * (g * jax.nn.sigmoid(g))).astype(BF16)
    o_ref[...] = out.reshape(BATCH, t_rows, D_SSM)


def _glu(y_slabs, g_nat, w_bf16, b_row):
    nat_spec = pl.BlockSpec((BATCH, TM_GLU // BATCH, D_SSM), lambda i: (0, i, 0))
    return pl.pallas_call(
        _glu_kernel,
        grid=(BATCH * SEQ // TM_GLU,),
        in_specs=[
            pl.BlockSpec((N_LANE_BLOCKS, TM_GLU, LANES), lambda i: (0, i, 0)),
            nat_spec,
            pl.BlockSpec((D_SSM, D_SSM), lambda i: (0, 0)),
            pl.BlockSpec((1, D_SSM), lambda i: (0, 0)),
        ],
        out_specs=nat_spec,
        out_shape=jax.ShapeDtypeStruct((BATCH, SEQ, D_SSM), BF16),
        scratch_shapes=[pltpu.VMEM((TM_GLU, D_SSM), F32)],
        compiler_params=pltpu.CompilerParams(
            dimension_semantics=("parallel",), vmem_limit_bytes=VMEM_LIMIT),
        name="glu",
    )(y_slabs, g_nat, w_bf16, b_row)


def _attn_kernel(lam_ref, q_ref, k_ref, v_ref, ga_ref, ng_ref, o_ref, *, post_scale):
    lam = lam_ref[0]
    lane = lax.broadcasted_iota(jnp.int32, (TQ, V_DIM), 1)
    zero = jnp.zeros((TQ, V_DIM), BF16)
    tiles = [(hh, r) for hh in range(HEADS_PER_STEP) for r in range(SEQ // TQ)]

    def scores(hh, r):
        cols = slice(hh * V_DIM, (hh + 1) * V_DIM)
        q = q_ref[r * TQ:(r + 1) * TQ, cols]
        k = k_ref[:, cols]
        return (lax.dot_general(jnp.where(lane < HEAD_DIM, q, zero), k, _NT,
                                preferred_element_type=F32),
                lax.dot_general(jnp.where(lane >= HEAD_DIM, q, zero), k, _NT,
                                preferred_element_type=F32))

    def exp_sum(s):
        e = jnp.exp2(s - jnp.max(s, axis=1, keepdims=True))
        return e, jnp.sum(e, axis=1, keepdims=True)

    s_next = scores(*tiles[0])
    for t, (hh, r) in enumerate(tiles):
        s1, s2 = s_next
        if t + 1 < len(tiles):
            s_next = scores(*tiles[t + 1])
        rows = slice(r * TQ, (r + 1) * TQ)
        cols = slice(hh * V_DIM, (hh + 1) * V_DIM)
        e1, l1 = exp_sum(s1)
        e2, l2 = exp_sum(s2)
        w = (e1 - (lam * l1 / l2) * e2).astype(BF16)
        o = jnp.dot(w, v_ref[:, cols], preferred_element_type=F32) / l1
        ms = jnp.mean(o * o, axis=1, keepdims=True)
        o = o * lax.rsqrt(ms + RMS_EPS) * ng_ref[...] * post_scale
        ga = ga_ref[rows, cols]
        o_ref[rows, cols] = (o * (ga * jax.nn.sigmoid(ga))).astype(BF16)


def _attention(lam, q, k, v, ga, ng_row, *, post_scale):
    blk = pl.BlockSpec((SEQ, HEADS_PER_STEP * V_DIM), lambda b, h: (b, h))
    return pl.pallas_call(
        functools.partial(_attn_kernel, post_scale=post_scale),
        grid=(BATCH, N_HEADS // HEADS_PER_STEP),
        in_specs=[
            pl.BlockSpec(memory_space=pltpu.SMEM),
            blk,
            blk,
            blk,
            blk,
            pl.BlockSpec((1, V_DIM), lambda b, h: (0, 0)),
        ],
        out_specs=blk,
        out_shape=jax.ShapeDtypeStruct((BATCH * SEQ, D_ATTN), BF16),
        compiler_params=pltpu.CompilerParams(
            dimension_semantics=("parallel", "parallel"),
            vmem_limit_bytes=VMEM_LIMIT),
        name="diff_attn",
    )(lam, q, k, v, ga, ng_row)


def _out_proj_kernel(ms_ref, ma_ref, w1_ref, w2_ref, x_ref, eg_ref, eb_ref, g_ref, b_ref,
                     o_ref, *maybe_ob_ref, pre_ln):
    for n in range(D_MODEL // PROJ_SLICE):
        cols = slice(n * PROJ_SLICE, (n + 1) * PROJ_SLICE)
        o_ref[:, cols] = (
            jnp.dot(ms_ref[...], w1_ref[:, cols], preferred_element_type=F32)
            + jnp.dot(ma_ref[...], w2_ref[:, cols], preferred_element_type=F32))
    x = x_ref[...]
    if pre_ln:
        x = _layer_norm(x, eg_ref[...], eb_ref[...])
    y = _layer_norm(DEEPNORM_ALPHA * x + o_ref[...], g_ref[...], b_ref[...])
    o_ref[...] = y
    for ob_ref in maybe_ob_ref:
        ob_ref[...] = y.astype(BF16)


def _out_proj(mix_ssm, mix_attn, w_bf16, layer, x2d, eg, eb, g, b, *, pre_ln, emit_bf16):
    n_rows = BATCH * SEQ
    row = lambda i: (0, 0)
    nat = lambda i: (i, 0)
    out_specs = [pl.BlockSpec((TM_OUT, D_MODEL), nat)]
    out_shape = [jax.ShapeDtypeStruct((n_rows, D_MODEL), F32)]
    if emit_bf16:
        out_specs.append(pl.BlockSpec((TM_OUT, D_MODEL), nat))
        out_shape.append(jax.ShapeDtypeStruct((n_rows, D_MODEL), BF16))
    return pl.pallas_call(
        functools.partial(_out_proj_kernel, pre_ln=pre_ln),
        grid=(n_rows // TM_OUT,),
        in_specs=[
            pl.BlockSpec((TM_OUT, D_SSM), nat),
            pl.BlockSpec((TM_OUT, D_ATTN), nat),
            pl.BlockSpec((None, D_SSM, D_MODEL), lambda i: (layer, 0, 0)),
            pl.BlockSpec((None, D_ATTN, D_MODEL), lambda i: (layer, 1, 0)),
            pl.BlockSpec((TM_OUT, D_MODEL), nat),
            pl.BlockSpec((1, D_MODEL), row),
            pl.BlockSpec((1, D_MODEL), row),
            pl.BlockSpec((1, D_MODEL), row),
            pl.BlockSpec((1, D_MODEL), row),
        ],
        out_specs=out_specs,
        out_shape=out_shape,
        compiler_params=pltpu.CompilerParams(
            dimension_semantics=("parallel",), vmem_limit_bytes=VMEM_LIMIT),
        name="out_proj",
    )(mix_ssm, mix_attn, w_bf16, w_bf16, x2d, eg, eb, g, b)


def _rotary_tables():
    pos = jnp.arange(SEQ, dtype=F32)
    inv_freq = ROPE_THETA ** (-jnp.arange(0, ROT_DIM, 2, dtype=F32) / ROT_DIM)
    ang = pos[:, None] * inv_freq[None, :]
    cos = jnp.cos(ang)
    sin = jnp.sin(ang)
    ones = jnp.ones((SEQ, HEAD_DIM - ROT_DIM), F32)
    zeros = jnp.zeros((SEQ, HEAD_DIM - ROT_DIM), F32)
    z8 = jnp.zeros((SEQ, ROT_HALF), F32)
    reps = LANES // HEAD_DIM
    cos_t = jnp.tile(jnp.concatenate([cos, cos, ones], axis=1), (1, reps))
    sin_a = jnp.tile(jnp.concatenate([-sin, z8, zeros], axis=1), (1, reps))
    sin_b = jnp.tile(jnp.concatenate([z8, sin, zeros], axis=1), (1, reps))
    return cos_t, sin_a, sin_b


def _ssm_tables(lam_re, lam_im, log_step, b_re, b_im, c_re, c_im):
    G, P, C, L = N_SSM_GROUPS, SSM_STATE, SSM_GROUP, CHUNK
    GB, NT, GT = N_LANE_BLOCKS, N_STATE_TILES, GROUPS_PER_TILE
    hi = lax.Precision.HIGHEST
    step = jnp.exp(log_step)[..., None]
    zr = lam_re * step
    zi = lam_im * step
    kpow = jnp.arange(L + 1, dtype=F32)[:, None, None, None]
    mag = jnp.exp(kpow * zr[None])
    pw = jnp.stack([mag * jnp.cos(kpow * zi[None]),
                    mag * jnp.sin(kpow * zi[None])])
    nr = pw[0, 1] - 1.0
    ni = pw[1, 1]
    den = lam_re * lam_re + lam_im * lam_im
    coef_re = (nr * lam_re + ni * lam_im) / den
    coef_im = (ni * lam_re - nr * lam_im) / den
    bb = jnp.stack([coef_re[..., None] * b_re - coef_im[..., None] * b_im,
                    coef_re[..., None] * b_im + coef_im[..., None] * b_re])

    m_re = pw[0, :L, ..., None] * bb[0][None] - pw[1, :L, ..., None] * bb[1][None]
    m_im = pw[0, :L, ..., None] * bb[1][None] + pw[1, :L, ..., None] * bb[0][None]
    taps = (jnp.einsum('dgop,kdgpi->kdgio', c_re, m_re, precision=hi)
            - jnp.einsum('dgop,kdgpi->kdgio', c_im, m_im, precision=hi))
    tap_lag = jnp.concatenate([taps[:0:-1, 1], (taps[0, 0] + taps[0, 1])[None], taps[1:, 0]])
    tapr = tap_lag.transpose(0, 1, 3, 2).reshape(N_LAGS, GB, LANES, C)
    tapr = tapr.transpose(1, 0, 2, 3).reshape(GB, N_LAGS * LANES, C).astype(BF16)

    pwr = pw.reshape(2, L + 1, 2, GB, NT, GT * P).transpose(3, 4, 0, 2, 1, 5)
    bbt = bb.reshape(2, 2, GB, NT, GT, P, C).transpose(2, 3, 0, 1, 6, 4, 5)
    bbt = jnp.tile(bbt.reshape(GB, NT, 2, 2, C, LANES), (1, 1, 1, 1, GROUPS_PER_BLOCK, 1))

    cc = jnp.stack([c_re, c_im]).reshape(2, 2, GB, LANES, P).transpose(2, 0, 1, 3, 4)
    cdup = jnp.concatenate([cc, cc], axis=-1)
    ar_l = jnp.arange(L)
    pt = jnp.stack([pw[:, ar_l + 1, 0], pw[:, L - ar_l, 1]], axis=2)
    pt = pt.reshape(2, L, 2, GB, GROUPS_PER_BLOCK, P).transpose(3, 1, 0, 2, 4, 5)
    pwt = jnp.concatenate([pt, pt], axis=-1)
    return tapr, pwr, bbt, cdup, pwt


def kernel(x, ln_emb_g, ln_emb_b, w_in, ssm_lam_re, ssm_lam_im, ssm_log_step, ssm_b_re,
           ssm_b_im, ssm_c_re, ssm_c_im, ssm_d, w_glu, b_glu, lambda_q1, lambda_k1,
           lambda_q2, lambda_k2, attn_norm_g, w_out, ln_g, ln_b):
    assert x.shape == (BATCH, SEQ, D_MODEL) and x.dtype == F32
    cos_t, sin_a, sin_b = _rotary_tables()
    eg = ln_emb_g.reshape(1, D_MODEL)
    eb = ln_emb_b.reshape(1, D_MODEL)
    x2d = x.reshape(BATCH * SEQ, D_MODEL)
    rep = (jnp.arange(LANES)[None, :] % SSM_GROUP == jnp.arange(SSM_GROUP)[:, None]).astype(BF16)

    w_in_bf16 = w_in.astype(BF16)
    w_out_bf16 = w_out.astype(BF16)
    xb = _emb_ln(x2d, eg, eb)
    for l in range(DEPTH):
        pre_ln = l == 0
        u_tb, gs_tb = _proj_pair(xb, w_in_bf16, l, 0, "ssm")
        q, k = _proj_pair(xb, w_in_bf16, l, 1, "qk", (cos_t, sin_a, sin_b))
        v, ga = _proj_pair(xb, w_in_bf16, l, 2, "vg")

        tapr, pwr, bbt, cdup, pwt = _ssm_tables(
            ssm_lam_re[l], ssm_lam_im[l], ssm_log_step[l], ssm_b_re[l], ssm_b_im[l],
            ssm_c_re[l], ssm_c_im[l])
        u5 = u_tb.reshape(N_LANE_BLOCKS, N_CHUNKS, CHUNK, BATCH, LANES)
        zflat, hstate = _ssm_state(u5, pwr, bbt)
        d_b = jnp.broadcast_to(ssm_d[l].reshape(N_LANE_BLOCKS, 1, LANES),
                               (N_LANE_BLOCKS, SUBLANES, LANES))
        y5 = _ssm_out(zflat, hstate, tapr, rep, cdup, pwt, u5, d_b)
        mix_ssm = _glu(y5.reshape(N_LANE_BLOCKS, SEQ * BATCH, LANES), gs_tb,
                       w_glu[l].astype(BF16), b_glu[l].reshape(1, D_SSM))

        lambda_init = 0.8 - 0.6 * math.exp(-0.3 * l)
        lam = (jnp.exp(jnp.sum(lambda_q1[l] * lambda_k1[l]))
               - jnp.exp(jnp.sum(lambda_q2[l] * lambda_k2[l])) + lambda_init)
        mix_attn = _attention(lam.reshape(1).astype(F32), q, k, v, ga,
                              attn_norm_g[l].reshape(1, V_DIM), post_scale=1.0 - lambda_init)

        outs = _out_proj(mix_ssm.reshape(BATCH * SEQ, D_SSM), mix_attn, w_out_bf16, l,
                         x2d, eg, eb, ln_g[l].reshape(1, D_MODEL), ln_b[l].reshape(1, D_MODEL),
                         pre_ln=pre_ln, emit_bf16=l + 1 < DEPTH)
        x2d = outs[0]
        if l + 1 < DEPTH:
            xb = outs[1]
    return x2d.reshape(BATCH, SEQ, D_MODEL)
```
